```python
import math
import jax, jax.numpy as jnp
from jax import lax
import numpy as np

D_MODEL = 1024
BATCH = 32
SEQ = 2048
DEPTH = 4

N_MIXERS = 3
HEAD_DIM = 64
N_HEADS = D_MODEL // (2 * HEAD_DIM)
Q_BLOCK = 128
ROPE_THETA = 10000.0
GMLP_WIDTH = 3 * D_MODEL // 2
GMLP_GROUPS = 8
GMLP_GW = GMLP_WIDTH // GMLP_GROUPS
CHUNK = 128
CONV_WIDTH = 3
D_FF = ((8 * D_MODEL + 3 * 256 - 1) // (3 * 256)) * 256
NORM_EPS = 1e-6
SUBLN_EPS = 1e-5
LN_EPS = 1e-5
NEG_INF = -1e30

kernel_name = "hybrid_diffattn_gmlp_shortconv_trunk"


def n_layers_of(kind):
    return len(range(kind, DEPTH, N_MIXERS))


def rms_norm(x, g, eps=NORM_EPS):
    xf = x.astype(jnp.float32)
    y = xf * lax.rsqrt(jnp.mean(xf * xf, axis=-1, keepdims=True) + eps)
    return (y * g.astype(jnp.float32)).astype(x.dtype)


def layer_norm(x, g, b, eps=LN_EPS):
    xf = x.astype(jnp.float32)
    mu = jnp.mean(xf, axis=-1, keepdims=True)
    var = jnp.mean(jnp.square(xf - mu), axis=-1, keepdims=True)
    y = (xf - mu) * lax.rsqrt(var + eps)
    return (y * g.astype(jnp.float32) + b.astype(jnp.float32)).astype(x.dtype)


def rope_tables(positions):
    inv_freq = 1.0 / (ROPE_THETA ** (jnp.arange(0, HEAD_DIM, 2, dtype=jnp.float32) / HEAD_DIM))
    ang = positions.astype(jnp.float32)[..., None] * inv_freq
    return jnp.cos(ang)[:, :, None, :], jnp.sin(ang)[:, :, None, :]


def apply_rope(x, cos, sin):
    xf = x.astype(jnp.float32)
    x1, x2 = jnp.split(xf, 2, axis=-1)
    return jnp.concatenate([x1 * cos - x2 * sin, x2 * cos + x1 * sin], axis=-1).astype(x.dtype)


def diff_attention(h, cos, sin, w_in, lam, subln, w_out, lambda_init):
    B, S, _ = h.shape
    q, k, v = jnp.split(h @ w_in, 3, axis=-1)
    q = apply_rope(q.reshape(B, S, 2 * N_HEADS, HEAD_DIM), cos, sin) * (HEAD_DIM ** -0.5)
    k = apply_rope(k.reshape(B, S, 2 * N_HEADS, HEAD_DIM), cos, sin)
    v = v.reshape(B, S, N_HEADS, 2 * HEAD_DIM)
    lamf = lam.astype(jnp.float32)
    lam_full = (jnp.exp(jnp.sum(lamf[0] * lamf[1])) - jnp.exp(jnp.sum(lamf[2] * lamf[3]))
                + lambda_init)
    outs = []
    for i in range(S // Q_BLOCK):
        kv_len = (i + 1) * Q_BLOCK
        qb = q[:, i * Q_BLOCK:kv_len]
        kb = k[:, :kv_len]
        vb = v[:, :kv_len]
        s = jnp.einsum('bqmd,bkmd->bmqk', qb, kb, preferred_element_type=jnp.float32)
        qpos = i * Q_BLOCK + jnp.arange(Q_BLOCK)
        mask = jnp.arange(kv_len)[None, :] <= qpos[:, None]
        p = jax.nn.softmax(jnp.where(mask, s, NEG_INF), axis=-1)
        p = p.reshape(B, N_HEADS, 2, Q_BLOCK, kv_len)
        a = p[:, :, 0] - lam_full * p[:, :, 1]
        outs.append(jnp.einsum('bhqk,bkhe->bqhe', a.astype(vb.dtype), vb))
    o = jnp.concatenate(outs, axis=1)
    o = rms_norm(o, subln, SUBLN_EPS) * (1.0 - lambda_init)
    return o.reshape(B, S, D_MODEL) @ w_out


def chunked_gmlp(h, w_in, b_in, ln_g, ln_b, w_s, b_s, w_out):
    B, S, _ = h.shape
    z = jax.nn.gelu(h @ w_in + b_in, approximate=False)
    u, v = jnp.split(z, 2, axis=-1)
    v = layer_norm(v, ln_g, ln_b).reshape(B, S // CHUNK, CHUNK, GMLP_GROUPS, GMLP_GW)
    causal = jnp.tril(jnp.ones((CHUNK, CHUNK), dtype=w_s.dtype))
    ws = w_s * causal[None]
    sv = jnp.einsum('gts,bnsgc->bntgc', ws, v) + b_s.T[None, None, :, :, None]
    return (u * sv.reshape(B, S, GMLP_WIDTH)) @ w_out


def short_conv(h, w_in, conv_w, w_out):
    gb, gc, xs = jnp.split(h @ w_in, 3, axis=-1)
    hc = gc * xs
    conv = lax.conv_general_dilated(
        hc, conv_w[:, None, :], window_strides=(1,), padding=[(CONV_WIDTH - 1, 0)],
        dimension_numbers=('NWC', 'WIO', 'NWC'), feature_group_count=D_MODEL)
    return (gb * conv) @ w_out


def swiglu(h, w_gate_up, w_down):
    g, u = jnp.split(h @ w_gate_up, 2, axis=-1)
    return (jax.nn.silu(g) * u) @ w_down


def setup_inputs(seed: int = 0) -> dict:
    key = jax.random.key(seed)
    ks = jax.random.split(key, 24)
    nA, nB, nC = n_layers_of(0), n_layers_of(1), n_layers_of(2)
    D = D_MODEL
    nrm = lambda k, shape, scale: jax.random.normal(k, shape, jnp.float32) * scale
    offset = jax.random.randint(ks[1], (BATCH, 1), 0, 4096, dtype=jnp.int32)
    return {
        "x": jax.random.normal(ks[0], (BATCH, SEQ, D), jnp.float32),
        "positions": offset + jnp.arange(SEQ, dtype=jnp.int32)[None, :],
        "mix_norm": 1.0 + nrm(ks[2], (DEPTH, D), 0.02),
        "ffn_norm": 1.0 + nrm(ks[3], (DEPTH, D), 0.02),
        "final_norm": 1.0 + nrm(ks[4], (D,), 0.02),
        "attn_w_in": nrm(ks[5], (nA, D, 3 * D), D ** -0.5),
        "attn_lambda": nrm(ks[6], (nA, 4, HEAD_DIM), 0.1),
        "attn_subln": 1.0 + nrm(ks[7], (nA, 2 * HEAD_DIM), 0.02),
        "attn_w_out": nrm(ks[8], (nA, D, D), D ** -0.5),
        "gmlp_w_in": nrm(ks[9], (nB, D, 2 * GMLP_WIDTH), D ** -0.5),
        "gmlp_b_in": nrm(ks[10], (nB, 2 * GMLP_WIDTH), 0.02),
        "gmlp_ln_g": 1.0 + nrm(ks[11], (nB, GMLP_WIDTH), 0.02),
        "gmlp_ln_b": nrm(ks[12], (nB, GMLP_WIDTH), 0.02),
        "gmlp_w_s": nrm(ks[13], (nB, GMLP_GROUPS, CHUNK, CHUNK), CHUNK ** -0.5),
        "gmlp_b_s": 1.0 + nrm(ks[14], (nB, GMLP_GROUPS, CHUNK), 0.02),
        "gmlp_w_out": nrm(ks[15], (nB, GMLP_WIDTH, D), GMLP_WIDTH ** -0.5),
        "conv_w_in": nrm(ks[16], (nC, D, 3 * D), D ** -0.5),
        "conv_w": nrm(ks[17], (nC, CONV_WIDTH, D), CONV_WIDTH ** -0.5),
        "conv_w_out": nrm(ks[18], (nC, D, D), D ** -0.5),
        "ffn_w_gate_up": nrm(ks[19], (DEPTH, D, 2 * D_FF), D ** -0.5),
        "ffn_w_down": nrm(ks[20], (DEPTH, D_FF, D), D_FF ** -0.5),
    }


def reference(x, positions, mix_norm, ffn_norm, final_norm,
              attn_w_in, attn_lambda, attn_subln, attn_w_out,
              gmlp_w_in, gmlp_b_in, gmlp_ln_g, gmlp_ln_b, gmlp_w_s, gmlp_b_s, gmlp_w_out,
              conv_w_in, conv_w, conv_w_out,
              ffn_w_gate_up, ffn_w_down):
    cos, sin = rope_tables(positions)
    for i in range(DEPTH):
        kind, j = i % N_MIXERS, i // N_MIXERS
        h = rms_norm(x, mix_norm[i])
        if kind == 0:
            lambda_init = 0.8 - 0.6 * math.exp(-0.3 * i)
            m = diff_attention(h, cos, sin, attn_w_in[j], attn_lambda[j], attn_subln[j],
                               attn_w_out[j], lambda_init)
        elif kind == 1:
            m = chunked_gmlp(h, gmlp_w_in[j], gmlp_b_in[j], gmlp_ln_g[j], gmlp_ln_b[j],
                             gmlp_w_s[j], gmlp_b_s[j], gmlp_w_out[j])
        else:
            m = short_conv(h, conv_w_in[j], conv_w[j], conv_w_out[j])
        x = x + m
        x = x + swiglu(rms_norm(x, ffn_norm[i]), ffn_w_gate_up[i], ffn_w_down[i])
    return rms_norm(x, final_norm)
```

```python
import functools
import math

import jax
import jax.numpy as jnp
from jax import lax
from jax.experimental import pallas as pl
from jax.experimental.pallas import tpu as pltpu

N_MIXERS = 3
HEAD_DIM = 64
ROPE_THETA = 10000.0
GMLP_GROUPS = 8
CHUNK = 128
CONV_WIDTH = 3
NORM_EPS = 1e-6
SUBLN_EPS = 1e-5
LN_EPS = 1e-5

LANES = 128
TOKEN_TILE = 512
Q_TILE = 256
FF_CHUNK = 256
VMEM_LIMIT = 56 * 1024 * 1024

_BF16 = jnp.bfloat16
_F32 = jnp.float32


def _dot(a, b):
    return jnp.dot(a, b, preferred_element_type=_F32)


def _dot_nt(a, b):
    return lax.dot_general(a, b, (((1,), (1,)), ((), ())), preferred_element_type=_F32)


def _rms(x, g, eps):
    return x * lax.rsqrt(jnp.mean(x * x, axis=-1, keepdims=True) + eps) * g


def _resident(shape):
    return pl.BlockSpec(shape, lambda *_: (0,) * len(shape), pipeline_mode=pl.Buffered(1))


def _params(semantics):
    return pltpu.CompilerParams(dimension_semantics=semantics, vmem_limit_bytes=VMEM_LIMIT)


def _rope_table_kernel(pos_ref, invf_ref, cos_ref, sin_ref):
    ang = pos_ref[...] * invf_ref[...]
    cos_ref[...] = jnp.cos(ang)
    sin_ref[...] = jnp.sin(ang)


def _rope_tables(positions):
    m = positions.size
    half = HEAD_DIM // 2
    per_row = LANES // half
    inv_freq = 1.0 / (ROPE_THETA ** (jnp.arange(0, HEAD_DIM, 2, dtype=_F32) / HEAD_DIM))
    pos = jnp.repeat(positions.astype(_F32).reshape(m // per_row, per_row), half, axis=1)
    invf = jnp.tile(inv_freq, per_row).reshape(1, LANES)
    rows = m // per_row
    tile = 2048
    cos, sin = pl.pallas_call(
        _rope_table_kernel,
        grid=(rows // tile,),
        in_specs=[pl.BlockSpec((tile, LANES), lambda i: (i, 0)), _resident((1, LANES))],
        out_specs=[pl.BlockSpec((tile, LANES), lambda i: (i, 0))] * 2,
        out_shape=[jax.ShapeDtypeStruct((rows, LANES), _F32)] * 2,
        compiler_params=_params(("parallel",)),
        name="rope_tables",
    )(pos, invf)
    cos = cos.reshape(m, half)
    sin = sin.reshape(m, half)
    cos_t = jnp.tile(cos, (1, LANES // half))
    sin_t = jnp.tile(jnp.concatenate([-sin, sin], axis=1), (1, LANES // HEAD_DIM))
    return cos_t, sin_t


def _qkv_kernel(x_ref, g_ref, w_ref, cos_ref, sin_ref, out_ref, *, d_model):
    h = _rms(x_ref[...], g_ref[...], NORM_EPS).astype(_BF16)
    cos = cos_ref[...]
    sin = sin_ref[...]
    lane = lax.broadcasted_iota(jnp.int32, (1, LANES), 1)
    first_half = (lane % HEAD_DIM) < (HEAD_DIM // 2)
    for part in range(3):
        y = _dot(h, w_ref[:, part * d_model:(part + 1) * d_model])
        if part == 2:
            out_ref[:, part * d_model:(part + 1) * d_model] = y.astype(_BF16)
            continue
        if part == 0:
            y = y * (HEAD_DIM ** -0.5)
        for c in range(d_model // LANES):
            yc = y[:, c * LANES:(c + 1) * LANES]
            ahead = pltpu.roll(yc, LANES - HEAD_DIM // 2, 1)
            behind = pltpu.roll(yc, HEAD_DIM // 2, 1)
            rot = jnp.where(first_half, ahead, behind)
            col = part * d_model + c * LANES
            out_ref[:, col:col + LANES] = (yc * cos + rot * sin).astype(_BF16)


def _qkv_proj(x, g, w, cos_t, sin_t):
    m, d = x.shape
    n = w.shape[1]
    tm = TOKEN_TILE
    return pl.pallas_call(
        functools.partial(_qkv_kernel, d_model=d),
        grid=(m // tm,),
        in_specs=[
            pl.BlockSpec((tm, d), lambda i: (i, 0)),
            _resident((1, d)),
            _resident((d, n)),
            pl.BlockSpec((tm, LANES), lambda i: (i, 0)),
            pl.BlockSpec((tm, LANES), lambda i: (i, 0)),
        ],
        out_specs=pl.BlockSpec((tm, n), lambda i: (i, 0)),
        out_shape=jax.ShapeDtypeStruct((m, n), _BF16),
        compiler_params=_params(("parallel",)),
        name="attn_qkv",
    )(x, g, w, cos_t, sin_t)


def _attn_kernel(q_ref, k_ref, v_ref, lam_ref, subln_ref, o_ref, *, seq, lambda_init):
    tq = Q_TILE
    lam = lam_ref[...]
    lam_a = jnp.sum(jnp.sum(lam[0:1] * lam[1:2], axis=1, keepdims=True), axis=0, keepdims=True)
    lam_b = jnp.sum(jnp.sum(lam[2:3] * lam[3:4], axis=1, keepdims=True), axis=0, keepdims=True)
    lam_full = jnp.exp(lam_a) - jnp.exp(lam_b) + lambda_init

    lane = lax.broadcasted_iota(jnp.int32, (1, LANES), 1)
    keep_first = (lane < HEAD_DIM).astype(_F32)
    keep_second = 1.0 - keep_first
    row = lax.broadcasted_iota(jnp.int32, (2 * tq, tq), 0)
    col = lax.broadcasted_iota(jnp.int32, (2 * tq, tq), 1)
    causal = col <= jnp.where(row >= tq, row - tq, row)
    subln = subln_ref[...] * (1.0 - lambda_init)

    for i in range(seq // tq):
        start = i * tq
        q = q_ref[0, start:start + tq, :].astype(_F32)
        qq = jnp.concatenate([q * keep_first, q * keep_second], axis=0).astype(_BF16)
        s_d = _dot_nt(qq, k_ref[0, start:start + tq, :])
        s_d = jnp.where(causal, s_d, -1e30)
        m = jnp.max(s_d, axis=1, keepdims=True)
        if i > 0:
            s_o = _dot_nt(qq, k_ref[0, 0:start, :])
            m = jnp.maximum(m, jnp.max(s_o, axis=1, keepdims=True))
        p_d = jnp.exp(s_d - m)
        l = jnp.sum(p_d, axis=1, keepdims=True)
        if i > 0:
            p_o = jnp.exp(s_o - m)
            l = l + jnp.sum(p_o, axis=1, keepdims=True)
        r = 1.0 / l
        r1 = r[:tq]
        r2 = r[tq:] * lam_full
        a_d = (p_d[:tq] * r1 - p_d[tq:] * r2).astype(_BF16)
        o = _dot(a_d, v_ref[0, start:start + tq, :])
        if i > 0:
            a_o = (p_o[:tq] * r1 - p_o[tq:] * r2).astype(_BF16)
            o = o + _dot(a_o, v_ref[0, 0:start, :])
        o = o * lax.rsqrt(jnp.mean(o * o, axis=1, keepdims=True) + SUBLN_EPS) * subln
        o_ref[0, start:start + tq, :] = o.astype(_BF16)


def _attention(qkv, lam, subln, lambda_init, batch, seq, d_model):
    n_heads = d_model // (2 * HEAD_DIM)
    qkv = qkv.reshape(batch, seq, 3 * d_model)
    blk = lambda off: pl.BlockSpec((1, seq, LANES), lambda b, h: (b, 0, off + h))
    out = pl.pallas_call(
        functools.partial(_attn_kernel, seq=seq, lambda_init=lambda_init),
        grid=(batch, n_heads),
        in_specs=[blk(0), blk(n_heads), blk(2 * n_heads),
                  _resident(lam.shape), _resident((1, LANES))],
        out_specs=pl.BlockSpec((1, seq, LANES), lambda b, h: (b, 0, h)),
        out_shape=jax.ShapeDtypeStruct((batch, seq, d_model), _BF16),
        compiler_params=_params(("parallel", "parallel")),
        name="diff_attn",
    )(qkv, qkv, qkv, lam, subln.reshape(1, LANES))
    return out.reshape(batch * seq, d_model)


def _ffn_kernel(*refs, has_proj, final, d_ff):
    refs = list(refs)
    x_ref = refs.pop(0)
    if has_proj:
        o_ref = refs.pop(0)
        wo_ref = refs.pop(0)
    g_ref, wgu_ref, wd_ref = refs.pop(0), refs.pop(0), refs.pop(0)
    if final:
        gf_ref = refs.pop(0)
    out_ref, act_ref = refs

    x = x_ref[...]
    if has_proj:
        x = x + _dot(o_ref[...], wo_ref[...])
    h = _rms(x, g_ref[...], NORM_EPS).astype(_BF16)
    fc = FF_CHUNK
    for j in range(d_ff // fc):
        gu = _dot(h, wgu_ref[:, 2 * j * fc:2 * (j + 1) * fc])
        gate = gu[:, :fc]
        up = gu[:, fc:]
        act = gate * (1.0 / (1.0 + jnp.exp(-gate))) * up
        act_ref[:, j * fc:(j + 1) * fc] = act.astype(_BF16)
    y = x + _dot(act_ref[...], wd_ref[...])
    if final:
        y = _rms(y, gf_ref[...], NORM_EPS)
    out_ref[...] = y


def _ffn(x, g, wgu, wd, proj=None, final_g=None):
    m, d = x.shape
    d_ff = wd.shape[0]
    tm = TOKEN_TILE
    row = lambda n: pl.BlockSpec((tm, n), lambda i: (i, 0))
    args, specs = [x], [row(d)]
    if proj is not None:
        o, wo = proj
        args += [o, wo]
        specs += [row(o.shape[1]), _resident(wo.shape)]
    args += [g, wgu, wd]
    specs += [_resident((1, d)), _resident(wgu.shape), _resident(wd.shape)]
    if final_g is not None:
        args.append(final_g)
        specs.append(_resident((1, d)))
    return pl.pallas_call(
        functools.partial(_ffn_kernel, has_proj=proj is not None,
                          final=final_g is not None, d_ff=d_ff),
        grid=(m // tm,),
        in_specs=specs,
        out_specs=row(d),
        out_shape=jax.ShapeDtypeStruct((m, d), _F32),
        scratch_shapes=[pltpu.VMEM((tm, d_ff), _BF16)],
        compiler_params=_params(("parallel",)),
        name="ffn",
    )(*args)


def _interleave_gate_up(w, d_ff):
    d = w.shape[0]
    n = d_ff // FF_CHUNK
    gate = w[:, :d_ff].reshape(d, n, FF_CHUNK)
    up = w[:, d_ff:].reshape(d, n, FF_CHUNK)
    return jnp.concatenate([gate, up], axis=2).reshape(d, 2 * d_ff)


def _gmlp_kernel(x_ref, g_ref, win_ref, bin_ref, lng_ref, lnb_ref, ws_ref, bs_ref, wout_ref,
                 out_ref, u_ref, v_ref, vn_ref, gated_ref, *, width):
    tm = x_ref.shape[0]
    x = x_ref[...]
    h = _rms(x, g_ref[...], NORM_EPS).astype(_BF16)
    nc = 512
    for j in range(2 * width // nc):
        z = _dot(h, win_ref[:, j * nc:(j + 1) * nc]) + bin_ref[:, j * nc:(j + 1) * nc]
        z = 0.5 * z * (1.0 + lax.erf(z * (1.0 / math.sqrt(2.0))))
        if j * nc < width:
            u_ref[:, j * nc:(j + 1) * nc] = z
        else:
            v_ref[:, j * nc - width:(j + 1) * nc - width] = z

    v = v_ref[...]
    mu = jnp.mean(v, axis=1, keepdims=True)
    vc = v - mu
    var = jnp.mean(vc * vc, axis=1, keepdims=True)
    vn_ref[...] = (vc * lax.rsqrt(var + LN_EPS) * lng_ref[...] + lnb_ref[...]).astype(_BF16)

    trow = lax.broadcasted_iota(jnp.int32, (CHUNK, CHUNK), 0)
    tcol = lax.broadcasted_iota(jnp.int32, (CHUNK, CHUNK), 1)
    tril = tcol <= trow
    ws = [jnp.where(tril, ws_ref[gi], 0.0).astype(_BF16) for gi in range(GMLP_GROUPS)]
    gw = width // GMLP_GROUPS
    low_half = lax.broadcasted_iota(jnp.int32, (1, LANES), 1) < (gw - LANES)
    for c in range(tm // CHUNK):
        rows = slice(c * CHUNK, (c + 1) * CHUNK)
        for p in range(GMLP_GROUPS // 2):
            base = 2 * p * gw
            sa = _dot(ws[2 * p], vn_ref[rows, base:base + 2 * LANES])
            sb = _dot(ws[2 * p + 1], vn_ref[rows, base + LANES:base + 3 * LANES])
            pieces = (sa[:, :LANES],
                      jnp.where(low_half, sa[:, LANES:], sb[:, :LANES]),
                      sb[:, LANES:])
            for t, sv in enumerate(pieces):
                cols = slice(base + t * LANES, base + (t + 1) * LANES)
                gated_ref[rows, cols] = (u_ref[rows, cols] * (sv + bs_ref[:, cols])).astype(_BF16)
    out_ref[...] = x + _dot(gated_ref[...], wout_ref[...])


def _gmlp(x, g, w_in, b_in, ln_g, ln_b, w_s, b_s, w_out):
    m, d = x.shape
    width = w_out.shape[0]
    tm = TOKEN_TILE
    gw = width // GMLP_GROUPS
    bias = jnp.repeat(b_s.T, gw, axis=1)
    row = lambda n: pl.BlockSpec((tm, n), lambda i: (i, 0))
    return pl.pallas_call(
        functools.partial(_gmlp_kernel, width=width),
        grid=(m // tm,),
        in_specs=[row(d), _resident((1, d)), _resident(w_in.shape), _resident((1, 2 * width)),
                  _resident((1, width)), _resident((1, width)), _resident(w_s.shape),
                  _resident(bias.shape), _resident(w_out.shape)],
        out_specs=row(d),
        out_shape=jax.ShapeDtypeStruct((m, d), _F32),
        scratch_shapes=[pltpu.VMEM((tm, width), _F32), pltpu.VMEM((tm, width), _F32),
                        pltpu.VMEM((tm, width), _BF16), pltpu.VMEM((tm, width), _BF16)],
        compiler_params=_params(("parallel",)),
        name="gmlp",
    )(x, g, w_in, b_in.reshape(1, -1), ln_g.reshape(1, -1), ln_b.reshape(1, -1), w_s, bias, w_out)


def _conv_kernel(x_ref, g_ref, win_ref, cw_ref, wout_ref, out_ref, hc_ref, *, d_model, tiles_per_seq):
    tm = x_ref.shape[0]
    pad = 8
    x = x_ref[...]
    h = _rms(x, g_ref[...], NORM_EPS).astype(_BF16)

    @pl.when(pl.program_id(0) % tiles_per_seq == 0)
    def _():
        hc_ref[0:pad, :] = jnp.zeros((pad, d_model), _F32)

    gate_c = _dot(h, win_ref[:, d_model:2 * d_model])
    xs = _dot(h, win_ref[:, 2 * d_model:3 * d_model])
    hc = gate_c * xs
    hc_ref[pad:pad + tm, :] = hc
    conv = (cw_ref[0:1, :] * hc_ref[pad - 2:pad - 2 + tm, :]
            + cw_ref[1:2, :] * hc_ref[pad - 1:pad - 1 + tm, :]
            + cw_ref[2:3, :] * hc)
    hc_ref[0:pad, :] = hc_ref[tm:tm + pad, :]
    gate_b = _dot(h, win_ref[:, 0:d_model])
    out_ref[...] = x + _dot((gate_b * conv).astype(_BF16), wout_ref[...])


def _short_conv(x, g, w_in, conv_w, w_out, seq):
    m, d = x.shape
    tm = TOKEN_TILE
    row = lambda n: pl.BlockSpec((tm, n), lambda i: (i, 0))
    return pl.pallas_call(
        functools.partial(_conv_kernel, d_model=d, tiles_per_seq=seq // tm),
        grid=(m // tm,),
        in_specs=[row(d), _resident((1, d)), _resident(w_in.shape), _resident(conv_w.shape),
                  _resident(w_out.shape)],
        out_specs=row(d),
        out_shape=jax.ShapeDtypeStruct((m, d), _F32),
        scratch_shapes=[pltpu.VMEM((tm + 8, d), _F32)],
        compiler_params=_params(("arbitrary",)),
        name="short_conv",
    )(x, g, w_in, conv_w, w_out)


def kernel(x, positions, mix_norm, ffn_norm, final_norm, attn_w_in, attn_lambda, attn_subln, attn_w_out, gmlp_w_in, gmlp_b_in, gmlp_ln_g, gmlp_ln_b, gmlp_w_s, gmlp_b_s, gmlp_w_out, conv_w_in, conv_w, conv_w_out, ffn_w_gate_up, ffn_w_down):
    batch, seq, d = x.shape
    depth = mix_norm.shape[0]
    d_ff = ffn_w_down.shape[1]
    assert CONV_WIDTH == conv_w.shape[1] and seq % TOKEN_TILE == 0 and seq % Q_TILE == 0
    assert d_ff % FF_CHUNK == 0 and TOKEN_TILE % CHUNK == 0

    xf = x.reshape(batch * seq, d)
    cos_t, sin_t = _rope_tables(positions)
    bf = lambda w: w.astype(_BF16)
    vec = lambda w: w.reshape(1, -1)

    for i in range(depth):
        kind, j = i % N_MIXERS, i // N_MIXERS
        proj = None
        if kind == 0:
            lambda_init = 0.8 - 0.6 * math.exp(-0.3 * i)
            qkv = _qkv_proj(xf, vec(mix_norm[i]), bf(attn_w_in[j]), cos_t, sin_t)
            o = _attention(qkv, attn_lambda[j], attn_subln[j], lambda_init, batch, seq, d)
            proj = (o, bf(attn_w_out[j]))
        elif kind == 1:
            xf = _gmlp(xf, vec(mix_norm[i]), bf(gmlp_w_in[j]), gmlp_b_in[j], gmlp_ln_g[j],
                       gmlp_ln_b[j], gmlp_w_s[j], gmlp_b_s[j], bf(gmlp_w_out[j]))
        else:
            xf = _short_conv(xf, vec(mix_norm[i]), bf(conv_w_in[j]), conv_w[j], bf(conv_w_out[j]), seq)
        wgu = bf(_interleave_gate_up(ffn_w_gate_up[i], d_ff))
        xf = _ffn(xf, vec(ffn_norm[i]), wgu, bf(ffn_w_down[i]), proj=proj,
                  final_g=vec(final_norm) if i == depth - 1 else None)
    return xf.reshape(batch, seq, d)
```

```python
import functools
import math

import jax
import jax.numpy as jnp
from jax import lax
from jax.experimental import pallas as pl
from jax.experimental.pallas import tpu as pltpu

N_MIXERS = 3
HEAD_DIM = 64
ROPE_THETA = 10000.0
GMLP_GROUPS = 8
CHUNK = 128
CONV_WIDTH = 3
NORM_EPS = 1e-6
SUBLN_EPS = 1e-5
LN_EPS = 1e-5

LANES = 128
TOKEN_TILE = 512
Q_TILE = 256
FF_CHUNK = 256
VMEM_LIMIT = 56 * 1024 * 1024

_BF16 = jnp.bfloat16
_F32 = jnp.float32


def _dot(a, b):
    return jnp.dot(a, b, preferred_element_type=_F32)


def _dot_nt(a, b):
    return lax.dot_general(a, b, (((1,), (1,)), ((), ())), preferred_element_type=_F32)


def _rms(x, g, eps):
    return x * lax.rsqrt(jnp.mean(x * x, axis=-1, keepdims=True) + eps) * g


def _resident(shape):
    return pl.BlockSpec(shape, lambda *_: (0,) * len(shape), pipeline_mode=pl.Buffered(1))


def _params(semantics):
    return pltpu.CompilerParams(dimension_semantics=semantics, vmem_limit_bytes=VMEM_LIMIT)


def _rope_table_kernel(pos_ref, invf_ref, cos_ref, sin_ref):
    ang = pos_ref[...] * invf_ref[...]
    cos_ref[...] = jnp.cos(ang)
    sin_ref[...] = jnp.sin(ang)


def _rope_tables(positions):
    m = positions.size
    half = HEAD_DIM // 2
    per_row = LANES // half
    inv_freq = 1.0 / (ROPE_THETA ** (jnp.arange(0, HEAD_DIM, 2, dtype=_F32) / HEAD_DIM))
    pos = positions.astype(_F32).reshape(m // per_row, per_row, 1)
    pos = jnp.broadcast_to(pos, (m // per_row, per_row, half)).reshape(m // per_row, LANES)
    invf = jnp.tile(inv_freq, per_row).reshape(1, LANES)
    rows = m // per_row
    tile = 2048
    cos, sin = pl.pallas_call(
        _rope_table_kernel,
        grid=(rows // tile,),
        in_specs=[pl.BlockSpec((tile, LANES), lambda i: (i, 0)), _resident((1, LANES))],
        out_specs=[pl.BlockSpec((tile, LANES), lambda i: (i, 0))] * 2,
        out_shape=[jax.ShapeDtypeStruct((rows, LANES), _F32)] * 2,
        compiler_params=_params(("parallel",)),
        name="rope_tables",
    )(pos, invf)
    cos = cos.reshape(m, half)
    sin = sin.reshape(m, half)
    cos_t = jnp.tile(cos, (1, LANES // half))
    sin_t = jnp.tile(jnp.concatenate([-sin, sin], axis=1), (1, LANES // HEAD_DIM))
    return cos_t, sin_t


def _qkv_kernel(x_ref, g_ref, w_ref, cos_ref, sin_ref, out_ref, *, d_model):
    h = _rms(x_ref[...], g_ref[...], NORM_EPS).astype(_BF16)
    cos = cos_ref[...]
    sin = sin_ref[...]
    lane = lax.broadcasted_iota(jnp.int32, (1, LANES), 1)
    first_half = (lane % HEAD_DIM) < (HEAD_DIM // 2)
    for part in range(3):
        y = _dot(h, w_ref[:, part * d_model:(part + 1) * d_model])
        if part == 2:
            out_ref[:, part * d_model:(part + 1) * d_model] = y.astype(_BF16)
            continue
        if part == 0:
            y = y * (HEAD_DIM ** -0.5 * math.log2(math.e))
        for c in range(d_model // LANES):
            yc = y[:, c * LANES:(c + 1) * LANES]
            ahead = pltpu.roll(yc, LANES - HEAD_DIM // 2, 1)
            behind = pltpu.roll(yc, HEAD_DIM // 2, 1)
            rot = jnp.where(first_half, ahead, behind)
            col = part * d_model + c * LANES
            out_ref[:, col:col + LANES] = (yc * cos + rot * sin).astype(_BF16)


def _qkv_proj(x, g, w, cos_t, sin_t):
    m, d = x.shape
    n = w.shape[1]
    tm = TOKEN_TILE
    return pl.pallas_call(
        functools.partial(_qkv_kernel, d_model=d),
        grid=(m // tm,),
        in_specs=[
            pl.BlockSpec((tm, d), lambda i: (i, 0)),
            _resident((1, d)),
            _resident((d, n)),
            pl.BlockSpec((tm, LANES), lambda i: (i, 0)),
            pl.BlockSpec((tm, LANES), lambda i: (i, 0)),
        ],
        out_specs=pl.BlockSpec((tm, n), lambda i: (i, 0)),
        out_shape=jax.ShapeDtypeStruct((m, n), _BF16),
        compiler_params=_params(("parallel",)),
        name="attn_qkv",
    )(x, g, w, cos_t, sin_t)


def _attn_kernel(q_ref, k_ref, v_ref, lam_ref, subln_ref, o_ref, s_ref, a_ref, *, seq, lambda_init):
    tq = Q_TILE
    lam = lam_ref[...]
    lam_a = jnp.sum(jnp.sum(lam[0:1] * lam[1:2], axis=1, keepdims=True), axis=0, keepdims=True)
    lam_b = jnp.sum(jnp.sum(lam[2:3] * lam[3:4], axis=1, keepdims=True), axis=0, keepdims=True)
    lam_full = jnp.exp(lam_a) - jnp.exp(lam_b) + lambda_init

    lane = lax.broadcasted_iota(jnp.int32, (1, LANES), 1)
    keep_first = (lane < HEAD_DIM).astype(_F32)
    keep_second = 1.0 - keep_first
    row = lax.broadcasted_iota(jnp.int32, (2 * tq, tq), 0)
    col = lax.broadcasted_iota(jnp.int32, (2 * tq, tq), 1)
    causal = col <= jnp.where(row >= tq, row - tq, row)
    subln = subln_ref[...] * (1.0 - lambda_init)

    def scores(i):
        kv = (i + 1) * tq
        q = q_ref[0, i * tq:kv, :].astype(_F32)
        qq = jnp.concatenate([q * keep_first, q * keep_second], axis=0).astype(_BF16)
        s_ref[i % 2, :, 0:kv] = _dot_nt(qq, k_ref[0, 0:kv, :])

    def finish(i):
        slot = i % 2
        start = i * tq
        kv = start + tq
        s_d = jnp.where(causal, s_ref[slot, :, start:kv], -1e30)
        m_fold = s_d
        for c in range(i):
            m_fold = jnp.maximum(m_fold, s_ref[slot, :, c * tq:(c + 1) * tq])
        m = jnp.max(m_fold, axis=1, keepdims=True)
        p_d = jnp.exp2(s_d - m)
        l_fold = p_d
        for c in range(i):
            p_c = jnp.exp2(s_ref[slot, :, c * tq:(c + 1) * tq] - m)
            s_ref[slot, :, c * tq:(c + 1) * tq] = p_c
            l_fold = l_fold + p_c
        l = jnp.sum(l_fold, axis=1, keepdims=True)
        ratio = lam_full * l[:tq] / l[tq:]
        a_ref[slot, :, start:kv] = (p_d[:tq] - p_d[tq:] * ratio).astype(_BF16)
        if i > 0:
            a_ref[slot, :, 0:start] = (s_ref[slot, 0:tq, 0:start]
                                       - s_ref[slot, tq:2 * tq, 0:start] * ratio).astype(_BF16)
        o = _dot(a_ref[slot, :, 0:kv], v_ref[0, 0:kv, :]) * (1.0 / l[:tq])
        o = o * lax.rsqrt(jnp.mean(o * o, axis=1, keepdims=True) + SUBLN_EPS) * subln
        o_ref[0, start:kv, :] = o.astype(_BF16)

    scores(0)
    for i in range(seq // tq):
        if i + 1 < seq // tq:
            scores(i + 1)
        finish(i)


def _attention(qkv, lam, subln, lambda_init, batch, seq, d_model):
    n_heads = d_model // (2 * HEAD_DIM)
    qkv = qkv.reshape(batch, seq, 3 * d_model)
    blk = lambda off: pl.BlockSpec((1, seq, LANES), lambda b, h: (b, 0, off + h))
    out = pl.pallas_call(
        functools.partial(_attn_kernel, seq=seq, lambda_init=lambda_init),
        grid=(batch, n_heads),
        in_specs=[blk(0), blk(n_heads), blk(2 * n_heads),
                  _resident(lam.shape), _resident((1, LANES))],
        out_specs=pl.BlockSpec((1, seq, LANES), lambda b, h: (b, 0, h)),
        out_shape=jax.ShapeDtypeStruct((batch, seq, d_model), _BF16),
        scratch_shapes=[pltpu.VMEM((2, 2 * Q_TILE, seq), _F32), pltpu.VMEM((2, Q_TILE, seq), _BF16)],
        compiler_params=_params(("parallel", "parallel")),
        name="diff_attn",
    )(qkv, qkv, qkv, lam, subln.reshape(1, LANES))
    return out.reshape(batch * seq, d_model)


def _ffn_kernel(*refs, has_proj, final, d_ff):
    refs = list(refs)
    x_ref = refs.pop(0)
    if has_proj:
        o_ref = refs.pop(0)
        wo_ref = refs.pop(0)
    g_ref, wgu_ref, wd_ref = refs.pop(0), refs.pop(0), refs.pop(0)
    if final:
        gf_ref = refs.pop(0)
    out_ref, act_ref = refs

    x = x_ref[...]
    if has_proj:
        x = x + _dot(o_ref[...], wo_ref[...])
    h = _rms(x, g_ref[...], NORM_EPS).astype(_BF16)
    fc = FF_CHUNK
    for j in range(d_ff // fc):
        gate = _dot(h, wgu_ref[:, j * fc:(j + 1) * fc])
        up = _dot(h, wgu_ref[:, d_ff + j * fc:d_ff + (j + 1) * fc])
        act = gate * (1.0 / (1.0 + jnp.exp(-gate))) * up
        act_ref[:, j * fc:(j + 1) * fc] = act.astype(_BF16)
    y = x + _dot(act_ref[...], wd_ref[...])
    if final:
        y = _rms(y, gf_ref[...], NORM_EPS)
    out_ref[...] = y


def _ffn(x, g, wgu, wd, proj=None, final_g=None):
    m, d = x.shape
    d_ff = wd.shape[0]
    tm = TOKEN_TILE
    row = lambda n: pl.BlockSpec((tm, n), lambda i: (i, 0))
    args, specs = [x], [row(d)]
    if proj is not None:
        o, wo = proj
        args += [o, wo]
        specs += [row(o.shape[1]), _resident(wo.shape)]
    args += [g, wgu, wd]
    specs += [_resident((1, d)), _resident(wgu.shape), _resident(wd.shape)]
    if final_g is not None:
        args.append(final_g)
        specs.append(_resident((1, d)))
    return pl.pallas_call(
        functools.partial(_ffn_kernel, has_proj=proj is not None,
                          final=final_g is not None, d_ff=d_ff),
        grid=(m // tm,),
        in_specs=specs,
        out_specs=row(d),
        out_shape=jax.ShapeDtypeStruct((m, d), _F32),
        scratch_shapes=[pltpu.VMEM((tm, d_ff), _BF16)],
        compiler_params=_params(("parallel",)),
        name="ffn",
    )(*args)


def _gmlp_kernel(x_ref, g_ref, win_ref, bin_ref, lng_ref, lnb_ref, ws_ref, bs_ref, wout_ref,
                 out_ref, u_ref, v_ref, vn_ref, gated_ref, *, width):
    tm = x_ref.shape[0]
    x = x_ref[...]
    h = _rms(x, g_ref[...], NORM_EPS).astype(_BF16)
    nc = 512
    for j in range(2 * width // nc):
        z = _dot(h, win_ref[:, j * nc:(j + 1) * nc]) + bin_ref[:, j * nc:(j + 1) * nc]
        z = 0.5 * z * (1.0 + lax.erf(z * (1.0 / math.sqrt(2.0))))
        if j * nc < width:
            u_ref[:, j * nc:(j + 1) * nc] = z
        else:
            v_ref[:, j * nc - width:(j + 1) * nc - width] = z

    v = v_ref[...]
    mu = jnp.mean(v, axis=1, keepdims=True)
    vc = v - mu
    var = jnp.mean(vc * vc, axis=1, keepdims=True)
    vn_ref[...] = (vc * lax.rsqrt(var + LN_EPS) * lng_ref[...] + lnb_ref[...]).astype(_BF16)

    trow = lax.broadcasted_iota(jnp.int32, (CHUNK, CHUNK), 0)
    tcol = lax.broadcasted_iota(jnp.int32, (CHUNK, CHUNK), 1)
    tril = tcol <= trow
    ws = [jnp.where(tril, ws_ref[gi], 0.0).astype(_BF16) for gi in range(GMLP_GROUPS)]
    gw = width // GMLP_GROUPS
    low_half = lax.broadcasted_iota(jnp.int32, (1, LANES), 1) < (gw - LANES)
    for c in range(tm // CHUNK):
        rows = slice(c * CHUNK, (c + 1) * CHUNK)
        for p in range(GMLP_GROUPS // 2):
            base = 2 * p * gw
            sa = _dot(ws[2 * p], vn_ref[rows, base:base + 2 * LANES])
            sb = _dot(ws[2 * p + 1], vn_ref[rows, base + LANES:base + 3 * LANES])
            pieces = (sa[:, :LANES],
                      jnp.where(low_half, sa[:, LANES:], sb[:, :LANES]),
                      sb[:, LANES:])
            for t, sv in enumerate(pieces):
                cols = slice(base + t * LANES, base + (t + 1) * LANES)
                gated_ref[rows, cols] = (u_ref[rows, cols] * (sv + bs_ref[:, cols])).astype(_BF16)
    out_ref[...] = x + _dot(gated_ref[...], wout_ref[...])


def _gmlp(x, g, w_in, b_in, ln_g, ln_b, w_s, b_s, w_out):
    m, d = x.shape
    width = w_out.shape[0]
    tm = TOKEN_TILE
    gw = width // GMLP_GROUPS
    bias = jnp.broadcast_to(b_s.T[:, :, None], (CHUNK, GMLP_GROUPS, gw)).reshape(CHUNK, width)
    row = lambda n: pl.BlockSpec((tm, n), lambda i: (i, 0))
    return pl.pallas_call(
        functools.partial(_gmlp_kernel, width=width),
        grid=(m // tm,),
        in_specs=[row(d), _resident((1, d)), _resident(w_in.shape), _resident((1, 2 * width)),
                  _resident((1, width)), _resident((1, width)), _resident(w_s.shape),
                  _resident(bias.shape), _resident(w_out.shape)],
        out_specs=row(d),
        out_shape=jax.ShapeDtypeStruct((m, d), _F32),
        scratch_shapes=[pltpu.VMEM((tm, width), _F32), pltpu.VMEM((tm, width), _F32),
                        pltpu.VMEM((tm, width), _BF16), pltpu.VMEM((tm, width), _BF16)],
        compiler_params=_params(("parallel",)),
        name="gmlp",
    )(x, g, w_in, b_in.reshape(1, -1), ln_g.reshape(1, -1), ln_b.reshape(1, -1), w_s, bias, w_out)


def _conv_kernel(x_ref, g_ref, win_ref, cw_ref, wout_ref, out_ref, hc_ref, *, d_model, tiles_per_seq):
    tm = x_ref.shape[0]
    pad = 8
    x = x_ref[...]
    h = _rms(x, g_ref[...], NORM_EPS).astype(_BF16)

    @pl.when(pl.program_id(0) % tiles_per_seq == 0)
    def _():
        hc_ref[0:pad, :] = jnp.zeros((pad, d_model), _F32)

    gate_c = _dot(h, win_ref[:, d_model:2 * d_model])
    xs = _dot(h, win_ref[:, 2 * d_model:3 * d_model])
    hc = gate_c * xs
    hc_ref[pad:pad + tm, :] = hc
    conv = (cw_ref[0:1, :] * hc_ref[pad - 2:pad - 2 + tm, :]
            + cw_ref[1:2, :] * hc_ref[pad - 1:pad - 1 + tm, :]
            + cw_ref[2:3, :] * hc)
    hc_ref[0:pad, :] = hc_ref[tm:tm + pad, :]
    gate_b = _dot(h, win_ref[:, 0:d_model])
    out_ref[...] = x + _dot((gate_b * conv).astype(_BF16), wout_ref[...])


def _short_conv(x, g, w_in, conv_w, w_out, seq):
    m, d = x.shape
    tm = TOKEN_TILE
    row = lambda n: pl.BlockSpec((tm, n), lambda i: (i, 0))
    return pl.pallas_call(
        functools.partial(_conv_kernel, d_model=d, tiles_per_seq=seq // tm),
        grid=(m // tm,),
        in_specs=[row(d), _resident((1, d)), _resident(w_in.shape), _resident(conv_w.shape),
                  _resident(w_out.shape)],
        out_specs=row(d),
        out_shape=jax.ShapeDtypeStruct((m, d), _F32),
        scratch_shapes=[pltpu.VMEM((tm + 8, d), _F32)],
        compiler_params=_params(("arbitrary",)),
        name="short_conv",
    )(x, g, w_in, conv_w, w_out)


def kernel(x, positions, mix_norm, ffn_norm, final_norm, attn_w_in, attn_lambda, attn_subln, attn_w_out, gmlp_w_in, gmlp_b_in, gmlp_ln_g, gmlp_ln_b, gmlp_w_s, gmlp_b_s, gmlp_w_out, conv_w_in, conv_w, conv_w_out, ffn_w_gate_up, ffn_w_down):
    batch, seq, d = x.shape
    depth = mix_norm.shape[0]
    d_ff = ffn_w_down.shape[1]
    assert CONV_WIDTH == conv_w.shape[1] and seq % TOKEN_TILE == 0 and seq % Q_TILE == 0
    assert d_ff % FF_CHUNK == 0 and TOKEN_TILE % CHUNK == 0

    xf = x.reshape(batch * seq, d)
    cos_t, sin_t = _rope_tables(positions)
    bf = lambda w: w.astype(_BF16)
    vec = lambda w: w.reshape(1, -1)

    for i in range(depth):
        kind, j = i % N_MIXERS, i // N_MIXERS
        proj = None
        if kind == 0:
            lambda_init = 0.8 - 0.6 * math.exp(-0.3 * i)
            qkv = _qkv_proj(xf, vec(mix_norm[i]), bf(attn_w_in[j]), cos_t, sin_t)
            o = _attention(qkv, attn_lambda[j], attn_subln[j], lambda_init, batch, seq, d)
            proj = (o, bf(attn_w_out[j]))
        elif kind == 1:
            xf = _gmlp(xf, vec(mix_norm[i]), bf(gmlp_w_in[j]), gmlp_b_in[j], gmlp_ln_g[j],
                       gmlp_ln_b[j], gmlp_w_s[j], gmlp_b_s[j], bf(gmlp_w_out[j]))
        else:
            xf = _short_conv(xf, vec(mix_norm[i]), bf(conv_w_in[j]), conv_w[j], bf(conv_w_out[j]), seq)
        xf = _ffn(xf, vec(ffn_norm[i]), bf(ffn_w_gate_up[i]), bf(ffn_w_down[i]), proj=proj,
                  final_g=vec(final_norm) if i == depth - 1 else None)
    return xf.reshape(batch, seq, d)
```

```python
import functools
import math

import jax
import jax.numpy as jnp
from jax import lax
from jax.experimental import pallas as pl
from jax.experimental.pallas import tpu as pltpu

N_MIXERS = 3
HEAD_DIM = 64
ROPE_THETA = 10000.0
GMLP_GROUPS = 8
CHUNK = 128
CONV_WIDTH = 3
NORM_EPS = 1e-6
SUBLN_EPS = 1e-5
LN_EPS = 1e-5

LANES = 128
TOKEN_TILE = 1024
Q_TILE = 256
HEADS_PER_STEP = 2
FF_CHUNK = 256
ROPE_ROWS = 2048
VMEM_LIMIT = 56 * 1024 * 1024

_BF16 = jnp.bfloat16
_F32 = jnp.float32


def _dot(a, b):
    return jnp.dot(a, b, preferred_element_type=_F32)


def _dot_nt(a, b):
    return lax.dot_general(a, b, (((1,), (1,)), ((), ())), preferred_element_type=_F32)


def _rms(x, g, eps):
    return x * lax.rsqrt(jnp.mean(x * x, axis=-1, keepdims=True) + eps) * g


def _resident(shape):
    return pl.BlockSpec(shape, lambda *_: (0,) * len(shape), pipeline_mode=pl.Buffered(1))


def _layer(stacked, j):
    _, rows, cols = stacked.shape
    return pl.BlockSpec((None, rows, cols), lambda *_: (j, 0, 0), pipeline_mode=pl.Buffered(1))


def _params(semantics):
    return pltpu.CompilerParams(dimension_semantics=semantics, vmem_limit_bytes=VMEM_LIMIT)


def _rope_table_kernel(pos_ref, invf_ref, cos_ref, sin_ref):
    half = HEAD_DIM // 2
    groups = LANES // half
    ang = pos_ref[...] * invf_ref[...]
    lane = lax.broadcasted_iota(jnp.int32, (1, LANES), 1)
    sign = jnp.where((lane % HEAD_DIM) < half, -1.0, 1.0)
    tables = ((jnp.cos(ang), cos_ref, None), (jnp.sin(ang), sin_ref, sign))
    for g in range(groups):
        keep = (lane // half == g).astype(_F32)
        for packed, out_ref, scale in tables:
            y = packed * keep
            y = y + pltpu.roll(y, half, 1)
            y = y + pltpu.roll(y, 2 * half, 1)
            out_ref[g] = y if scale is None else y * scale


def _rope_tables(positions):
    m = positions.size
    half = HEAD_DIM // 2
    groups = LANES // half
    rows = m // groups
    inv_freq = 1.0 / (ROPE_THETA ** (jnp.arange(0, HEAD_DIM, 2, dtype=_F32) / HEAD_DIM))
    pos = positions.astype(_F32).reshape(groups, rows).T
    pos = jnp.broadcast_to(pos[:, :, None], (rows, groups, half)).reshape(rows, LANES)
    invf = jnp.tile(inv_freq, groups).reshape(1, LANES)
    tile = ROPE_ROWS
    cos, sin = pl.pallas_call(
        _rope_table_kernel,
        grid=(rows // tile,),
        in_specs=[pl.BlockSpec((tile, LANES), lambda i: (i, 0)), _resident((1, LANES))],
        out_specs=[pl.BlockSpec((groups, tile, LANES), lambda i: (0, i, 0))] * 2,
        out_shape=[jax.ShapeDtypeStruct((groups, rows, LANES), _F32)] * 2,
        compiler_params=_params(("parallel",)),
        name="rope_tables",
    )(pos, invf)
    return cos.reshape(m, LANES), sin.reshape(m, LANES)


def _qkv_kernel(x_ref, g_ref, w_ref, cos_ref, sin_ref, out_ref, *, d_model):
    h = _rms(x_ref[...], g_ref[...], NORM_EPS).astype(_BF16)
    cos = cos_ref[...]
    sin = sin_ref[...]
    lane = lax.broadcasted_iota(jnp.int32, (1, LANES), 1)
    first_half = (lane % HEAD_DIM) < (HEAD_DIM // 2)
    for part in range(3):
        y = _dot(h, w_ref[:, part * d_model:(part + 1) * d_model])
        if part == 2:
            out_ref[:, part * d_model:(part + 1) * d_model] = y.astype(_BF16)
            continue
        if part == 0:
            y = y * (HEAD_DIM ** -0.5 * math.log2(math.e))
        for c in range(d_model // LANES):
            yc = y[:, c * LANES:(c + 1) * LANES]
            ahead = pltpu.roll(yc, LANES - HEAD_DIM // 2, 1)
            behind = pltpu.roll(yc, HEAD_DIM // 2, 1)
            rot = jnp.where(first_half, ahead, behind)
            col = part * d_model + c * LANES
            out_ref[:, col:col + LANES] = (yc * cos + rot * sin).astype(_BF16)


def _qkv_proj(x, g, w_stack, j, cos_t, sin_t):
    m, d = x.shape
    n = w_stack.shape[2]
    tm = TOKEN_TILE
    return pl.pallas_call(
        functools.partial(_qkv_kernel, d_model=d),
        grid=(m // tm,),
        in_specs=[
            pl.BlockSpec((tm, d), lambda i: (i, 0)),
            _resident((1, d)),
            _layer(w_stack, j),
            pl.BlockSpec((tm, LANES), lambda i: (i, 0)),
            pl.BlockSpec((tm, LANES), lambda i: (i, 0)),
        ],
        out_specs=pl.BlockSpec((tm, n), lambda i: (i, 0)),
        out_shape=jax.ShapeDtypeStruct((m, n), _BF16),
        compiler_params=_params(("parallel",)),
        name="attn_qkv",
    )(x, g, w_stack, cos_t, sin_t)


def _attn_kernel(q_ref, k_ref, v_ref, lam_ref, subln_ref, o_ref, s_ref, a_ref, *, seq, lambda_init):
    tq = Q_TILE
    lam = lam_ref[...]
    lam_a = jnp.sum(jnp.sum(lam[0:1] * lam[1:2], axis=1, keepdims=True), axis=0, keepdims=True)
    lam_b = jnp.sum(jnp.sum(lam[2:3] * lam[3:4], axis=1, keepdims=True), axis=0, keepdims=True)
    lam_full = jnp.exp(lam_a) - jnp.exp(lam_b) + lambda_init

    lane = lax.broadcasted_iota(jnp.int32, (1, LANES), 1)
    keep_first = (lane < HEAD_DIM).astype(_F32)
    keep_second = 1.0 - keep_first
    row = lax.broadcasted_iota(jnp.int32, (2 * tq, tq), 0)
    col = lax.broadcasted_iota(jnp.int32, (2 * tq, tq), 1)
    causal = col <= jnp.where(row >= tq, row - tq, row)
    subln = subln_ref[...] * (1.0 - lambda_init)

    def scores(n, hh, i):
        kv = (i + 1) * tq
        cols = slice(hh * LANES, (hh + 1) * LANES)
        q = q_ref[0, i * tq:kv, cols].astype(_F32)
        qq = jnp.concatenate([q * keep_first, q * keep_second], axis=0).astype(_BF16)
        s_ref[n % 2, :, 0:kv] = _dot_nt(qq, k_ref[0, 0:kv, cols])

    def finish(n, hh, i):
        slot = n % 2
        start = i * tq
        kv = start + tq
        cols = slice(hh * LANES, (hh + 1) * LANES)
        s_d = jnp.where(causal, s_ref[slot, :, start:kv], -1e30)
        m_fold = s_d
        for c in range(i):
            m_fold = jnp.maximum(m_fold, s_ref[slot, :, c * tq:(c + 1) * tq])
        m = jnp.max(m_fold, axis=1, keepdims=True)
        p_d = jnp.exp2(s_d - m)
        l_fold = p_d
        for c in range(i):
            p_c = jnp.exp2(s_ref[slot, :, c * tq:(c + 1) * tq] - m)
            s_ref[slot, :, c * tq:(c + 1) * tq] = p_c
            l_fold = l_fold + p_c
        l = jnp.sum(l_fold, axis=1, keepdims=True)
        ratio = lam_full * l[:tq] / l[tq:]
        a_ref[slot, :, start:kv] = (p_d[:tq] - p_d[tq:] * ratio).astype(_BF16)
        if i > 0:
            a_ref[slot, :, 0:start] = (s_ref[slot, 0:tq, 0:start]
                                       - s_ref[slot, tq:2 * tq, 0:start] * ratio).astype(_BF16)
        o = _dot(a_ref[slot, :, 0:kv], v_ref[0, 0:kv, cols]) * (1.0 / l[:tq])
        o = o * lax.rsqrt(jnp.mean(o * o, axis=1, keepdims=True) + SUBLN_EPS) * subln
        o_ref[0, start:kv, cols] = o.astype(_BF16)

    work = [(hh, i) for hh in range(HEADS_PER_STEP) for i in range(seq // tq)]
    scores(0, *work[0])
    for n, item in enumerate(work):
        if n + 1 < len(work):
            scores(n + 1, *work[n + 1])
        finish(n, *item)


def _attention(qkv, lam_stack, subln, j, lambda_init, batch, seq, d_model):
    n_steps = d_model // (2 * HEAD_DIM) // HEADS_PER_STEP
    width = HEADS_PER_STEP * LANES
    qkv = qkv.reshape(batch, seq, 3 * d_model)
    blk = lambda off: pl.BlockSpec((1, seq, width), lambda b, h: (b, 0, off + h))
    out = pl.pallas_call(
        functools.partial(_attn_kernel, seq=seq, lambda_init=lambda_init),
        grid=(batch, n_steps),
        in_specs=[blk(0), blk(n_steps), blk(2 * n_steps),
                  _layer(lam_stack, j), _resident((1, LANES))],
        out_specs=pl.BlockSpec((1, seq, width), lambda b, h: (b, 0, h)),
        out_shape=jax.ShapeDtypeStruct((batch, seq, d_model), _BF16),
        scratch_shapes=[pltpu.VMEM((2, 2 * Q_TILE, seq), _F32), pltpu.VMEM((2, Q_TILE, seq), _BF16)],
        compiler_params=_params(("parallel", "parallel")),
        name="diff_attn",
    )(qkv, qkv, qkv, lam_stack, subln.reshape(1, LANES))
    return out.reshape(batch * seq, d_model)


def _ffn_kernel(*refs, has_proj, final, d_ff):
    refs = list(refs)
    x_ref = refs.pop(0)
    if has_proj:
        o_ref = refs.pop(0)
        wo_ref = refs.pop(0)
    g_ref, wgu_ref, wd_ref = refs.pop(0), refs.pop(0), refs.pop(0)
    if final:
        gf_ref = refs.pop(0)
    out_ref, act_ref = refs

    x = x_ref[...]
    if has_proj:
        x = x + _dot(o_ref[...], wo_ref[...])
    h = _rms(x, g_ref[...], NORM_EPS).astype(_BF16)
    fc = FF_CHUNK
    for j in range(d_ff // fc):
        gate = _dot(h, wgu_ref[:, j * fc:(j + 1) * fc])
        up = _dot(h, wgu_ref[:, d_ff + j * fc:d_ff + (j + 1) * fc])
        act = gate * (1.0 / (1.0 + jnp.exp(-gate))) * up
        act_ref[:, j * fc:(j + 1) * fc] = act.astype(_BF16)
    y = x + _dot(act_ref[...], wd_ref[...])
    if final:
        y = _rms(y, gf_ref[...], NORM_EPS)
    out_ref[...] = y


def _ffn(x, g, wgu_stack, wd_stack, layer, proj=None, final_g=None):
    m, d = x.shape
    d_ff = wd_stack.shape[1]
    tm = TOKEN_TILE
    row = lambda n: pl.BlockSpec((tm, n), lambda i: (i, 0))
    args, specs = [x], [row(d)]
    if proj is not None:
        o, wo_stack, j = proj
        args += [o, wo_stack]
        specs += [row(o.shape[1]), _layer(wo_stack, j)]
    args += [g, wgu_stack, wd_stack]
    specs += [_resident((1, d)), _layer(wgu_stack, layer), _layer(wd_stack, layer)]
    if final_g is not None:
        args.append(final_g)
        specs.append(_resident((1, d)))
    return pl.pallas_call(
        functools.partial(_ffn_kernel, has_proj=proj is not None,
                          final=final_g is not None, d_ff=d_ff),
        grid=(m // tm,),
        in_specs=specs,
        out_specs=row(d),
        out_shape=jax.ShapeDtypeStruct((m, d), _F32),
        scratch_shapes=[pltpu.VMEM((tm, d_ff), _BF16)],
        compiler_params=_params(("parallel",)),
        name="ffn",
    )(*args)


def _gmlp_kernel(x_ref, g_ref, win_ref, bin_ref, lng_ref, lnb_ref, ws_ref, bs_ref, wout_ref,
                 out_ref, u_ref, v_ref, vn_ref, gated_ref, *, width):
    tm = x_ref.shape[0]
    x = x_ref[...]
    h = _rms(x, g_ref[...], NORM_EPS).astype(_BF16)
    nc = 512
    for j in range(2 * width // nc):
        z = _dot(h, win_ref[:, j * nc:(j + 1) * nc]) + bin_ref[:, j * nc:(j + 1) * nc]
        z = 0.5 * z * (1.0 + lax.erf(z * (1.0 / math.sqrt(2.0))))
        if j * nc < width:
            u_ref[:, j * nc:(j + 1) * nc] = z
        else:
            v_ref[:, j * nc - width:(j + 1) * nc - width] = z

    v = v_ref[...]
    mu = jnp.mean(v, axis=1, keepdims=True)
    vc = v - mu
    var = jnp.mean(vc * vc, axis=1, keepdims=True)
    vn_ref[...] = (vc * lax.rsqrt(var + LN_EPS) * lng_ref[...] + lnb_ref[...]).astype(_BF16)

    trow = lax.broadcasted_iota(jnp.int32, (CHUNK, CHUNK), 0)
    tcol = lax.broadcasted_iota(jnp.int32, (CHUNK, CHUNK), 1)
    tril = tcol <= trow
    ws = [jnp.where(tril, ws_ref[gi], 0.0).astype(_BF16) for gi in range(GMLP_GROUPS)]
    gw = width // GMLP_GROUPS
    low_half = lax.broadcasted_iota(jnp.int32, (1, LANES), 1) < (gw - LANES)
    for c in range(tm // CHUNK):
        rows = slice(c * CHUNK, (c + 1) * CHUNK)
        for p in range(GMLP_GROUPS // 2):
            base = 2 * p * gw
            sa = _dot(ws[2 * p], vn_ref[rows, base:base + 2 * LANES])
            sb = _dot(ws[2 * p + 1], vn_ref[rows, base + LANES:base + 3 * LANES])
            pieces = (sa[:, :LANES],
                      jnp.where(low_half, sa[:, LANES:], sb[:, :LANES]),
                      sb[:, LANES:])
            for t, sv in enumerate(pieces):
                cols = slice(base + t * LANES, base + (t + 1) * LANES)
                gated_ref[rows, cols] = (u_ref[rows, cols] * (sv + bs_ref[:, cols])).astype(_BF16)
    out_ref[...] = x + _dot(gated_ref[...], wout_ref[...])


def _gmlp(x, g, w_in_stack, b_in, ln_g, ln_b, w_s, b_s, w_out_stack, j):
    m, d = x.shape
    width = w_out_stack.shape[1]
    tm = TOKEN_TILE
    gw = width // GMLP_GROUPS
    bias = jnp.broadcast_to(b_s.T[:, :, None], (CHUNK, GMLP_GROUPS, gw)).reshape(CHUNK, width)
    row = lambda n: pl.BlockSpec((tm, n), lambda i: (i, 0))
    return pl.pallas_call(
        functools.partial(_gmlp_kernel, width=width),
        grid=(m // tm,),
        in_specs=[row(d), _resident((1, d)), _layer(w_in_stack, j), _resident((1, 2 * width)),
                  _resident((1, width)), _resident((1, width)), _resident(w_s.shape),
                  _resident(bias.shape), _layer(w_out_stack, j)],
        out_specs=row(d),
        out_shape=jax.ShapeDtypeStruct((m, d), _F32),
        scratch_shapes=[pltpu.VMEM((tm, width), _F32), pltpu.VMEM((tm, width), _F32),
                        pltpu.VMEM((tm, width), _BF16), pltpu.VMEM((tm, width), _BF16)],
        compiler_params=_params(("parallel",)),
        name="gmlp",
    )(x, g, w_in_stack, b_in.reshape(1, -1), ln_g.reshape(1, -1), ln_b.reshape(1, -1), w_s, bias,
      w_out_stack)


def _conv_kernel(x_ref, g_ref, win_ref, cw_ref, wout_ref, out_ref, hc_ref, *, d_model, tiles_per_seq):
    tm = x_ref.shape[0]
    pad = 8
    x = x_ref[...]
    h = _rms(x, g_ref[...], NORM_EPS).astype(_BF16)

    @pl.when(pl.program_id(0) % tiles_per_seq == 0)
    def _():
        hc_ref[0:pad, :] = jnp.zeros((pad, d_model), _F32)

    gate_c = _dot(h, win_ref[:, d_model:2 * d_model])
    xs = _dot(h, win_ref[:, 2 * d_model:3 * d_model])
    hc = gate_c * xs
    hc_ref[pad:pad + tm, :] = hc
    conv = (cw_ref[0:1, :] * hc_ref[pad - 2:pad - 2 + tm, :]
            + cw_ref[1:2, :] * hc_ref[pad - 1:pad - 1 + tm, :]
            + cw_ref[2:3, :] * hc)
    hc_ref[0:pad, :] = hc_ref[tm:tm + pad, :]
    gate_b = _dot(h, win_ref[:, 0:d_model])
    out_ref[...] = x + _dot((gate_b * conv).astype(_BF16), wout_ref[...])


def _short_conv(x, g, w_in_stack, conv_w, w_out_stack, j, seq):
    m, d = x.shape
    tm = TOKEN_TILE
    row = lambda n: pl.BlockSpec((tm, n), lambda i: (i, 0))
    return pl.pallas_call(
        functools.partial(_conv_kernel, d_model=d, tiles_per_seq=seq // tm),
        grid=(m // tm,),
        in_specs=[row(d), _resident((1, d)), _layer(w_in_stack, j), _resident(conv_w.shape),
                  _layer(w_out_stack, j)],
        out_specs=row(d),
        out_shape=jax.ShapeDtypeStruct((m, d), _F32),
        scratch_shapes=[pltpu.VMEM((tm + 8, d), _F32)],
        compiler_params=_params(("arbitrary",)),
        name="short_conv",
    )(x, g, w_in_stack, conv_w, w_out_stack)


def kernel(x, positions, mix_norm, ffn_norm, final_norm, attn_w_in, attn_lambda, attn_subln, attn_w_out, gmlp_w_in, gmlp_b_in, gmlp_ln_g, gmlp_ln_b, gmlp_w_s, gmlp_b_s, gmlp_w_out, conv_w_in, conv_w, conv_w_out, ffn_w_gate_up, ffn_w_down):
    batch, seq, d = x.shape
    depth = mix_norm.shape[0]
    d_ff = ffn_w_down.shape[1]
    assert CONV_WIDTH == conv_w.shape[1] and seq % TOKEN_TILE == 0 and seq % Q_TILE == 0
    assert d_ff % FF_CHUNK == 0 and TOKEN_TILE % CHUNK == 0
    assert (batch * seq) % (ROPE_ROWS * LANES // (HEAD_DIM // 2)) == 0
    assert d // (2 * HEAD_DIM) % HEADS_PER_STEP == 0

    xf = x.reshape(batch * seq, d)
    cos_t, sin_t = _rope_tables(positions)
    vec = lambda w: w.reshape(1, -1)
    attn_w_in, attn_w_out = attn_w_in.astype(_BF16), attn_w_out.astype(_BF16)
    gmlp_w_in, gmlp_w_out = gmlp_w_in.astype(_BF16), gmlp_w_out.astype(_BF16)
    conv_w_in, conv_w_out = conv_w_in.astype(_BF16), conv_w_out.astype(_BF16)
    ffn_w_gate_up, ffn_w_down = ffn_w_gate_up.astype(_BF16), ffn_w_down.astype(_BF16)

    for i in range(depth):
        kind, j = i % N_MIXERS, i // N_MIXERS
        proj = None
        if kind == 0:
            lambda_init = 0.8 - 0.6 * math.exp(-0.3 * i)
            qkv = _qkv_proj(xf, vec(mix_norm[i]), attn_w_in, j, cos_t, sin_t)
            o = _attention(qkv, attn_lambda, attn_subln[j], j, lambda_init, batch, seq, d)
            proj = (o, attn_w_out, j)
        elif kind == 1:
            xf = _gmlp(xf, vec(mix_norm[i]), gmlp_w_in, gmlp_b_in[j], gmlp_ln_g[j], gmlp_ln_b[j],
                       gmlp_w_s[j], gmlp_b_s[j], gmlp_w_out, j)
        else:
            xf = _short_conv(xf, vec(mix_norm[i]), conv_w_in, conv_w[j], conv_w_out, j, seq)
        xf = _ffn(xf, vec(ffn_norm[i]), ffn_w_gate_up, ffn_w_down, i, proj=proj,
                  final_g=vec(final_norm) if i == depth - 1 else None)
    return xf.reshape(batch, seq, d)
```

```python
import functools
import math

import jax
import jax.numpy as jnp
from jax import lax
from jax.experimental import pallas as pl
from jax.experimental.pallas import tpu as pltpu

N_MIXERS = 3
HEAD_DIM = 64
ROPE_THETA = 10000.0
GMLP_GROUPS = 8
CHUNK = 128
CONV_WIDTH = 3
NORM_EPS = 1e-6
SUBLN_EPS = 1e-5
LN_EPS = 1e-5

LANES = 128
TOKEN_TILE = 1024
Q_TILE = 256
HEADS_PER_STEP = 2
FF_CHUNK = 256
ROPE_ROWS = 2048
VMEM_LIMIT = 56 * 1024 * 1024

_BF16 = jnp.bfloat16
_F32 = jnp.float32


def _dot(a, b):
    return jnp.dot(a, b, preferred_element_type=_F32)


def _dot_nt(a, b):
    return lax.dot_general(a, b, (((1,), (1,)), ((), ())), preferred_element_type=_F32)


def _rms(x, g, eps):
    return x * lax.rsqrt(jnp.mean(x * x, axis=-1, keepdims=True) + eps) * g


def _resident(shape):
    return pl.BlockSpec(shape, lambda *_: (0,) * len(shape), pipeline_mode=pl.Buffered(1))


def _layer(stacked, j):
    _, rows, cols = stacked.shape
    return pl.BlockSpec((None, rows, cols), lambda *_: (j, 0, 0), pipeline_mode=pl.Buffered(1))


def _params(semantics):
    return pltpu.CompilerParams(dimension_semantics=semantics, vmem_limit_bytes=VMEM_LIMIT)


def _rope_table_kernel(pos_ref, invf_ref, cos_ref, sin_ref):
    half = HEAD_DIM // 2
    groups = LANES // half
    ang = pos_ref[...] * invf_ref[...]
    lane = lax.broadcasted_iota(jnp.int32, (1, LANES), 1)
    sign = jnp.where((lane % HEAD_DIM) < half, -1.0, 1.0)
    tables = ((jnp.cos(ang), cos_ref, None), (jnp.sin(ang), sin_ref, sign))
    for g in range(groups):
        keep = (lane // half == g).astype(_F32)
        for packed, out_ref, scale in tables:
            y = packed * keep
            y = y + pltpu.roll(y, half, 1)
            y = y + pltpu.roll(y, 2 * half, 1)
            out_ref[g] = y if scale is None else y * scale


def _rope_tables(positions):
    m = positions.size
    half = HEAD_DIM // 2
    groups = LANES // half
    rows = m // groups
    inv_freq = 1.0 / (ROPE_THETA ** (jnp.arange(0, HEAD_DIM, 2, dtype=_F32) / HEAD_DIM))
    pos = positions.astype(_F32).reshape(groups, rows).T
    pos = jnp.broadcast_to(pos[:, :, None], (rows, groups, half)).reshape(rows, LANES)
    invf = jnp.tile(inv_freq, groups).reshape(1, LANES)
    tile = ROPE_ROWS
    cos, sin = pl.pallas_call(
        _rope_table_kernel,
        grid=(rows // tile,),
        in_specs=[pl.BlockSpec((tile, LANES), lambda i: (i, 0)), _resident((1, LANES))],
        out_specs=[pl.BlockSpec((groups, tile, LANES), lambda i: (0, i, 0))] * 2,
        out_shape=[jax.ShapeDtypeStruct((groups, rows, LANES), _F32)] * 2,
        compiler_params=_params(("parallel",)),
        name="rope_tables",
    )(pos, invf)
    return cos.reshape(m, LANES), sin.reshape(m, LANES)


def _qkv_kernel(x_ref, g_ref, w_ref, cos_ref, sin_ref, out_ref, *, d_model):
    h = _rms(x_ref[...], g_ref[...], NORM_EPS).astype(_BF16)
    cos = cos_ref[...]
    sin = sin_ref[...]
    lane = lax.broadcasted_iota(jnp.int32, (1, LANES), 1)
    first_half = (lane % HEAD_DIM) < (HEAD_DIM // 2)
    for part in range(3):
        y = _dot(h, w_ref[:, part * d_model:(part + 1) * d_model])
        if part == 2:
            out_ref[:, part * d_model:(part + 1) * d_model] = y.astype(_BF16)
            continue
        if part == 0:
            y = y * (HEAD_DIM ** -0.5 * math.log2(math.e))
        for c in range(d_model // LANES):
            yc = y[:, c * LANES:(c + 1) * LANES]
            ahead = pltpu.roll(yc, LANES - HEAD_DIM // 2, 1)
            behind = pltpu.roll(yc, HEAD_DIM // 2, 1)
            rot = jnp.where(first_half, ahead, behind)
            col = part * d_model + c * LANES
            out_ref[:, col:col + LANES] = (yc * cos + rot * sin).astype(_BF16)


def _qkv_proj(x, g, w_stack, j, cos_t, sin_t):
    m, d = x.shape
    n = w_stack.shape[2]
    tm = TOKEN_TILE
    return pl.pallas_call(
        functools.partial(_qkv_kernel, d_model=d),
        grid=(m // tm,),
        in_specs=[
            pl.BlockSpec((tm, d), lambda i: (i, 0)),
            _resident((1, d)),
            _layer(w_stack, j),
            pl.BlockSpec((tm, LANES), lambda i: (i, 0)),
            pl.BlockSpec((tm, LANES), lambda i: (i, 0)),
        ],
        out_specs=pl.BlockSpec((tm, n), lambda i: (i, 0)),
        out_shape=jax.ShapeDtypeStruct((m, n), _BF16),
        compiler_params=_params(("parallel",)),
        name="attn_qkv",
    )(x, g, w_stack, cos_t, sin_t)


def _attn_kernel(q_ref, k_ref, v_ref, lam_ref, subln_ref, o_ref, s_ref, vt_ref, *, seq, lambda_init):
    tq = Q_TILE
    lam = lam_ref[...]
    lam_a = jnp.sum(jnp.sum(lam[0:1] * lam[1:2], axis=1, keepdims=True), axis=0, keepdims=True)
    lam_b = jnp.sum(jnp.sum(lam[2:3] * lam[3:4], axis=1, keepdims=True), axis=0, keepdims=True)
    lam_full = jnp.exp(lam_a) - jnp.exp(lam_b) + lambda_init

    lane = lax.broadcasted_iota(jnp.int32, (1, LANES), 1)
    keep_first = (lane < HEAD_DIM).astype(_F32)
    keep_second = 1.0 - keep_first
    key = lax.broadcasted_iota(jnp.int32, (tq, 2 * tq), 0)
    qry = lax.broadcasted_iota(jnp.int32, (tq, 2 * tq), 1)
    causal = key <= jnp.where(qry >= tq, qry - tq, qry)
    subln = subln_ref[...] * (1.0 - lambda_init)
    for hh in range(HEADS_PER_STEP):
        vt_ref[hh] = v_ref[0, :, hh * LANES:(hh + 1) * LANES].astype(_F32).T.astype(_BF16)

    def scores(n, hh, i):
        kv = (i + 1) * tq
        cols = slice(hh * LANES, (hh + 1) * LANES)
        q = q_ref[0, i * tq:kv, cols].astype(_F32)
        qq = jnp.concatenate([q * keep_first, q * keep_second], axis=0).astype(_BF16)
        s_ref[n % 2, 0:kv, :] = _dot_nt(k_ref[0, 0:kv, cols], qq)

    def finish(n, hh, i):
        slot = n % 2
        start = i * tq
        kv = start + tq
        cols = slice(hh * LANES, (hh + 1) * LANES)
        s_d = jnp.where(causal, s_ref[slot, start:kv, :], -1e30)
        m = jnp.max(s_d, axis=0, keepdims=True)
        if i > 0:
            m = jnp.maximum(m, jnp.max(s_ref[slot, 0:start, :], axis=0, keepdims=True))
        p_d = jnp.exp2(s_d - m)
        l = jnp.sum(p_d, axis=0, keepdims=True)
        if i > 0:
            p_o = jnp.exp2(s_ref[slot, 0:start, :] - m)
            l = l + jnp.sum(p_o, axis=0, keepdims=True)
        ratio = lam_full * l[:, :tq] / l[:, tq:]
        a_d = (p_d[:, :tq] - p_d[:, tq:] * ratio).astype(_BF16)
        o_t = _dot(vt_ref[hh, :, start:kv], a_d)
        if i > 0:
            a_o = (p_o[:, :tq] - p_o[:, tq:] * ratio).astype(_BF16)
            o_t = o_t + _dot(vt_ref[hh, :, 0:start], a_o)
        o = (o_t * (1.0 / l[:, :tq])).T
        o = o * lax.rsqrt(jnp.mean(o * o, axis=1, keepdims=True) + SUBLN_EPS) * subln
        o_ref[0, start:kv, cols] = o.astype(_BF16)

    work = [(hh, i) for hh in range(HEADS_PER_STEP) for i in range(seq // tq)]
    scores(0, *work[0])
    for n, item in enumerate(work):
        if n + 1 < len(work):
            scores(n + 1, *work[n + 1])
        finish(n, *item)


def _attention(qkv, lam_stack, subln, j, lambda_init, batch, seq, d_model):
    n_steps = d_model // (2 * HEAD_DIM) // HEADS_PER_STEP
    width = HEADS_PER_STEP * LANES
    qkv = qkv.reshape(batch, seq, 3 * d_model)
    blk = lambda off: pl.BlockSpec((1, seq, width), lambda b, h: (b, 0, off + h))
    out = pl.pallas_call(
        functools.partial(_attn_kernel, seq=seq, lambda_init=lambda_init),
        grid=(batch, n_steps),
        in_specs=[blk(0), blk(n_steps), blk(2 * n_steps),
                  _layer(lam_stack, j), _resident((1, LANES))],
        out_specs=pl.BlockSpec((1, seq, width), lambda b, h: (b, 0, h)),
        out_shape=jax.ShapeDtypeStruct((batch, seq, d_model), _BF16),
        scratch_shapes=[pltpu.VMEM((2, seq, 2 * Q_TILE), _F32),
                        pltpu.VMEM((HEADS_PER_STEP, LANES, seq), _BF16)],
        compiler_params=_params(("parallel", "parallel")),
        name="diff_attn",
    )(qkv, qkv, qkv, lam_stack, subln.reshape(1, LANES))
    return out.reshape(batch * seq, d_model)


def _ffn_kernel(*refs, has_proj, final, d_ff):
    refs = list(refs)
    x_ref = refs.pop(0)
    if has_proj:
        o_ref = refs.pop(0)
        wo_ref = refs.pop(0)
    g_ref, wgu_ref, wd_ref = refs.pop(0), refs.pop(0), refs.pop(0)
    if final:
        gf_ref = refs.pop(0)
    out_ref, act_ref = refs

    x = x_ref[...]
    if has_proj:
        x = x + _dot(o_ref[...], wo_ref[...])
    h = _rms(x, g_ref[...], NORM_EPS).astype(_BF16)
    fc = FF_CHUNK
    for j in range(d_ff // fc):
        gate = _dot(h, wgu_ref[:, j * fc:(j + 1) * fc])
        up = _dot(h, wgu_ref[:, d_ff + j * fc:d_ff + (j + 1) * fc])
        act = gate * (1.0 / (1.0 + jnp.exp(-gate))) * up
        act_ref[:, j * fc:(j + 1) * fc] = act.astype(_BF16)
    y = x + _dot(act_ref[...], wd_ref[...])
    if final:
        y = _rms(y, gf_ref[...], NORM_EPS)
    out_ref[...] = y


def _ffn(x, g, wgu_stack, wd_stack, layer, proj=None, final_g=None):
    m, d = x.shape
    d_ff = wd_stack.shape[1]
    tm = TOKEN_TILE
    row = lambda n: pl.BlockSpec((tm, n), lambda i: (i, 0))
    args, specs = [x], [row(d)]
    if proj is not None:
        o, wo_stack, j = proj
        args += [o, wo_stack]
        specs += [row(o.shape[1]), _layer(wo_stack, j)]
    args += [g, wgu_stack, wd_stack]
    specs += [_resident((1, d)), _layer(wgu_stack, layer), _layer(wd_stack, layer)]
    if final_g is not None:
        args.append(final_g)
        specs.append(_resident((1, d)))
    return pl.pallas_call(
        functools.partial(_ffn_kernel, has_proj=proj is not None,
                          final=final_g is not None, d_ff=d_ff),
        grid=(m // tm,),
        in_specs=specs,
        out_specs=row(d),
        out_shape=jax.ShapeDtypeStruct((m, d), _F32),
        scratch_shapes=[pltpu.VMEM((tm, d_ff), _BF16)],
        compiler_params=_params(("parallel",)),
        name="ffn",
    )(*args)


def _gmlp_kernel(x_ref, g_ref, win_ref, bin_ref, lng_ref, lnb_ref, ws_ref, bs_ref, wout_ref,
                 out_ref, u_ref, v_ref, vn_ref, gated_ref, *, width):
    tm = x_ref.shape[0]
    nc = 512
    gw = width // GMLP_GROUPS
    trow = lax.broadcasted_iota(jnp.int32, (CHUNK, CHUNK), 0)
    tcol = lax.broadcasted_iota(jnp.int32, (CHUNK, CHUNK), 1)
    tril = tcol <= trow
    ws = [jnp.where(tril, ws_ref[gi], 0.0).astype(_BF16) for gi in range(GMLP_GROUPS)]
    low_half = lax.broadcasted_iota(jnp.int32, (1, LANES), 1) < (gw - LANES)

    x = x_ref[...]
    h = _rms(x, g_ref[...], NORM_EPS).astype(_BF16)

    def in_proj(col):
        z = _dot(h, win_ref[:, col:col + nc]) + bin_ref[:, col:col + nc]
        return 0.5 * z * (1.0 + lax.erf(z * (1.0 / math.sqrt(2.0))))

    for col in range(0, width, nc):
        v_ref[:, col:col + nc] = in_proj(width + col)
    v = v_ref[...]
    mu = jnp.mean(v, axis=1, keepdims=True)
    vc = v - mu
    var = jnp.mean(vc * vc, axis=1, keepdims=True)
    vn_ref[...] = (vc * lax.rsqrt(var + LN_EPS) * lng_ref[...] + lnb_ref[...]).astype(_BF16)
    for col in range(0, width, nc):
        u_ref[:, col:col + nc] = in_proj(col)

    for c in range(tm // CHUNK):
        rows = slice(c * CHUNK, (c + 1) * CHUNK)
        for p in range(GMLP_GROUPS // 2):
            base = 2 * p * gw
            sa = _dot(ws[2 * p], vn_ref[rows, base:base + 2 * LANES])
            sb = _dot(ws[2 * p + 1], vn_ref[rows, base + LANES:base + 3 * LANES])
            pieces = (sa[:, :LANES],
                      jnp.where(low_half, sa[:, LANES:], sb[:, :LANES]),
                      sb[:, LANES:])
            for t, sv in enumerate(pieces):
                cols = slice(base + t * LANES, base + (t + 1) * LANES)
                gated_ref[rows, cols] = (u_ref[rows, cols] * (sv + bs_ref[:, cols])).astype(_BF16)
    out_ref[...] = x + _dot(gated_ref[...], wout_ref[...])


def _gmlp(x, g, w_in_stack, b_in, ln_g, ln_b, w_s, b_s, w_out_stack, j):
    m, d = x.shape
    width = w_out_stack.shape[1]
    tm = TOKEN_TILE
    gw = width // GMLP_GROUPS
    assert w_s.shape == (GMLP_GROUPS, CHUNK, CHUNK) and 2 * gw == 3 * LANES
    bias = jnp.broadcast_to(b_s.T[:, :, None], (CHUNK, GMLP_GROUPS, gw)).reshape(CHUNK, width)
    row = lambda n: pl.BlockSpec((tm, n), lambda i: (i, 0))
    return pl.pallas_call(
        functools.partial(_gmlp_kernel, width=width),
        grid=(m // tm,),
        in_specs=[row(d), _resident((1, d)), _layer(w_in_stack, j), _resident((1, 2 * width)),
                  _resident((1, width)), _resident((1, width)), _resident(w_s.shape),
                  _resident(bias.shape), _layer(w_out_stack, j)],
        out_specs=row(d),
        out_shape=jax.ShapeDtypeStruct((m, d), _F32),
        scratch_shapes=[pltpu.VMEM((tm, width), _F32), pltpu.VMEM((tm, width), _F32),
                        pltpu.VMEM((tm, width), _BF16), pltpu.VMEM((tm, width), _BF16)],
        compiler_params=_params(("parallel",)),
        name="gmlp",
    )(x, g, w_in_stack, b_in.reshape(1, -1), ln_g.reshape(1, -1), ln_b.reshape(1, -1), w_s, bias,
      w_out_stack)


def _conv_kernel(x_ref, g_ref, win_ref, cw_ref, wout_ref, out_ref, hc_ref, *, d_model, tiles_per_seq):
    tm = x_ref.shape[0]
    pad = 8

    @pl.when(pl.program_id(0) % tiles_per_seq == 0)
    def _():
        hc_ref[0:pad, :] = jnp.zeros((pad, d_model), _F32)

    x = x_ref[...]
    h = _rms(x, g_ref[...], NORM_EPS).astype(_BF16)
    gate_c = _dot(h, win_ref[:, d_model:2 * d_model])
    xs = _dot(h, win_ref[:, 2 * d_model:3 * d_model])
    hc = gate_c * xs
    hc_ref[pad:pad + tm, :] = hc
    conv = (cw_ref[0:1, :] * hc_ref[pad - 2:pad - 2 + tm, :]
            + cw_ref[1:2, :] * hc_ref[pad - 1:pad - 1 + tm, :]
            + cw_ref[2:3, :] * hc)
    hc_ref[0:pad, :] = hc_ref[tm:tm + pad, :]
    gate_b = _dot(h, win_ref[:, 0:d_model])
    out_ref[...] = x + _dot((gate_b * conv).astype(_BF16), wout_ref[...])


def _short_conv(x, g, w_in_stack, conv_w, w_out_stack, j, seq):
    m, d = x.shape
    tm = TOKEN_TILE
    row = lambda n: pl.BlockSpec((tm, n), lambda i: (i, 0))
    return pl.pallas_call(
        functools.partial(_conv_kernel, d_model=d, tiles_per_seq=seq // tm),
        grid=(m // tm,),
        in_specs=[row(d), _resident((1, d)), _layer(w_in_stack, j), _resident(conv_w.shape),
                  _layer(w_out_stack, j)],
        out_specs=row(d),
        out_shape=jax.ShapeDtypeStruct((m, d), _F32),
        scratch_shapes=[pltpu.VMEM((tm + 8, d), _F32)],
        compiler_params=_params(("arbitrary",)),
        name="short_conv",
    )(x, g, w_in_stack, conv_w, w_out_stack)


def kernel(x, positions, mix_norm, ffn_norm, final_norm, attn_w_in, attn_lambda, attn_subln, attn_w_out, gmlp_w_in, gmlp_b_in, gmlp_ln_g, gmlp_ln_b, gmlp_w_s, gmlp_b_s, gmlp_w_out, conv_w_in, conv_w, conv_w_out, ffn_w_gate_up, ffn_w_down):
    batch, seq, d = x.shape
    depth = mix_norm.shape[0]
    d_ff = ffn_w_down.shape[1]
    assert CONV_WIDTH == conv_w.shape[1] and seq % TOKEN_TILE == 0 and seq % Q_TILE == 0
    assert d_ff % FF_CHUNK == 0 and TOKEN_TILE % CHUNK == 0
    assert (batch * seq) % (ROPE_ROWS * LANES // (HEAD_DIM // 2)) == 0
    assert d // (2 * HEAD_DIM) % HEADS_PER_STEP == 0

    xf = x.reshape(batch * seq, d)
    cos_t, sin_t = _rope_tables(positions)
    vec = lambda w: w.reshape(1, -1)
    attn_w_in, attn_w_out = attn_w_in.astype(_BF16), attn_w_out.astype(_BF16)
    gmlp_w_in, gmlp_w_out = gmlp_w_in.astype(_BF16), gmlp_w_out.astype(_BF16)
    conv_w_in, conv_w_out = conv_w_in.astype(_BF16), conv_w_out.astype(_BF16)
    ffn_w_gate_up, ffn_w_down = ffn_w_gate_up.astype(_BF16), ffn_w_down.astype(_BF16)

    for i in range(depth):
        kind, j = i % N_MIXERS, i // N_MIXERS
        proj = None
        if kind == 0:
            lambda_init = 0.8 - 0.6 * math.exp(-0.3 * i)
            qkv = _qkv_proj(xf, vec(mix_norm[i]), attn_w_in, j, cos_t, sin_t)
            o = _attention(qkv, attn_lambda, attn_subln[j], j, lambda_init, batch, seq, d)
            proj = (o, attn_w_out, j)
        elif kind == 1:
            xf = _gmlp(xf, vec(mix_norm[i]), gmlp_w_in, gmlp_b_in[j], gmlp_ln_g[j], gmlp_ln_b[j],
                       gmlp_w_s[j], gmlp_b_s[j], gmlp_w_out, j)
        else:
            xf = _short_conv(xf, vec(mix_norm[i]), conv_w_in, conv_w[j], conv_w_out, j, seq)
        xf = _ffn(xf, vec(ffn_norm[i]), ffn_w_gate_up, ffn_w_down, i, proj=proj,
                  final_g=vec(final_norm) if i == depth - 1 else None)
    return xf.reshape(batch, seq, d)
```

```python
import functools
import math

import jax
import jax.numpy as jnp
from jax import lax
from jax.experimental import pallas as pl
from jax.experimental.pallas import tpu as pltpu

N_MIXERS = 3
HEAD_DIM = 64
ROPE_THETA = 10000.0
GMLP_GROUPS = 8
CHUNK = 128
CONV_WIDTH = 3
NORM_EPS = 1e-6
SUBLN_EPS = 1e-5
LN_EPS = 1e-5

LANES = 128
TOKEN_TILE = 1024
Q_TILE = 256
HEADS_PER_STEP = 2
FF_CHUNK = 256
ROPE_ROWS = 2048
VMEM_LIMIT = 56 * 1024 * 1024

_BF16 = jnp.bfloat16
_F32 = jnp.float32


def _dot(a, b):
    return jnp.dot(a, b, preferred_element_type=_F32)


def _dot_nt(a, b):
    return lax.dot_general(a, b, (((1,), (1,)), ((), ())), preferred_element_type=_F32)


def _rms(x, g, eps):
    return x * lax.rsqrt(jnp.mean(x * x, axis=-1, keepdims=True) + eps) * g


def _resident(shape):
    return pl.BlockSpec(shape, lambda *_: (0,) * len(shape), pipeline_mode=pl.Buffered(1))


def _layer(stacked, j):
    _, rows, cols = stacked.shape
    return pl.BlockSpec((None, rows, cols), lambda *_: (j, 0, 0), pipeline_mode=pl.Buffered(1))


def _params(semantics):
    return pltpu.CompilerParams(dimension_semantics=semantics, vmem_limit_bytes=VMEM_LIMIT)


def _rope_table_kernel(pos_ref, invf_ref, cos_ref, sin_ref):
    half = HEAD_DIM // 2
    groups = LANES // half
    ang = pos_ref[...] * invf_ref[...]
    lane = lax.broadcasted_iota(jnp.int32, (1, LANES), 1)
    sign = jnp.where((lane % HEAD_DIM) < half, -1.0, 1.0)
    tables = ((jnp.cos(ang), cos_ref, None), (jnp.sin(ang), sin_ref, sign))
    for g in range(groups):
        keep = (lane // half == g).astype(_F32)
        for packed, out_ref, scale in tables:
            y = packed * keep
            y = y + pltpu.roll(y, half, 1)
            y = y + pltpu.roll(y, 2 * half, 1)
            out_ref[g] = y if scale is None else y * scale


def _rope_tables(positions):
    m = positions.size
    half = HEAD_DIM // 2
    groups = LANES // half
    rows = m // groups
    inv_freq = 1.0 / (ROPE_THETA ** (jnp.arange(0, HEAD_DIM, 2, dtype=_F32) / HEAD_DIM))
    pos = positions.astype(_F32).reshape(groups, rows).T
    pos = jnp.broadcast_to(pos[:, :, None], (rows, groups, half)).reshape(rows, LANES)
    invf = jnp.tile(inv_freq, groups).reshape(1, LANES)
    tile = ROPE_ROWS
    cos, sin = pl.pallas_call(
        _rope_table_kernel,
        grid=(rows // tile,),
        in_specs=[pl.BlockSpec((tile, LANES), lambda i: (i, 0)), _resident((1, LANES))],
        out_specs=[pl.BlockSpec((groups, tile, LANES), lambda i: (0, i, 0))] * 2,
        out_shape=[jax.ShapeDtypeStruct((groups, rows, LANES), _F32)] * 2,
        compiler_params=_params(("parallel",)),
        name="rope_tables",
    )(pos, invf)
    return cos.reshape(m, LANES), sin.reshape(m, LANES)


def _qkv_kernel(x_ref, g_ref, w_ref, cos_ref, sin_ref, out_ref, *, d_model):
    h = _rms(x_ref[...], g_ref[...], NORM_EPS).astype(_BF16)
    cos = cos_ref[...]
    sin = sin_ref[...]
    lane = lax.broadcasted_iota(jnp.int32, (1, LANES), 1)
    first_half = (lane % HEAD_DIM) < (HEAD_DIM // 2)
    for part in range(3):
        y = _dot(h, w_ref[:, part * d_model:(part + 1) * d_model])
        if part == 2:
            out_ref[:, part * d_model:(part + 1) * d_model] = y.astype(_BF16)
            continue
        if part == 0:
            y = y * (HEAD_DIM ** -0.5 * math.log2(math.e))
        for c in range(d_model // LANES):
            yc = y[:, c * LANES:(c + 1) * LANES]
            ahead = pltpu.roll(yc, LANES - HEAD_DIM // 2, 1)
            behind = pltpu.roll(yc, HEAD_DIM // 2, 1)
            rot = jnp.where(first_half, ahead, behind)
            col = part * d_model + c * LANES
            out_ref[:, col:col + LANES] = (yc * cos + rot * sin).astype(_BF16)


def _qkv_proj(x, g, w_stack, j, cos_t, sin_t):
    m, d = x.shape
    n = w_stack.shape[2]
    tm = TOKEN_TILE
    return pl.pallas_call(
        functools.partial(_qkv_kernel, d_model=d),
        grid=(m // tm,),
        in_specs=[
            pl.BlockSpec((tm, d), lambda i: (i, 0)),
            _resident((1, d)),
            _layer(w_stack, j),
            pl.BlockSpec((tm, LANES), lambda i: (i, 0)),
            pl.BlockSpec((tm, LANES), lambda i: (i, 0)),
        ],
        out_specs=pl.BlockSpec((tm, n), lambda i: (i, 0)),
        out_shape=jax.ShapeDtypeStruct((m, n), _BF16),
        compiler_params=_params(("parallel",)),
        name="attn_qkv",
    )(x, g, w_stack, cos_t, sin_t)


def _attn_kernel(q_ref, k_ref, v_ref, lam_ref, subln_ref, o_ref, s_ref, vt_ref, *, seq, lambda_init):
    tq = Q_TILE
    lam = lam_ref[...]
    lam_a = jnp.sum(jnp.sum(lam[0:1] * lam[1:2], axis=1, keepdims=True), axis=0, keepdims=True)
    lam_b = jnp.sum(jnp.sum(lam[2:3] * lam[3:4], axis=1, keepdims=True), axis=0, keepdims=True)
    lam_full = jnp.exp(lam_a) - jnp.exp(lam_b) + lambda_init

    lane = lax.broadcasted_iota(jnp.int32, (1, LANES), 1)
    keep_first = (lane < HEAD_DIM).astype(_F32)
    keep_second = 1.0 - keep_first
    key = lax.broadcasted_iota(jnp.int32, (tq, 2 * tq), 0)
    qry = lax.broadcasted_iota(jnp.int32, (tq, 2 * tq), 1)
    tri = key <= jnp.where(qry >= tq, qry - tq, qry)
    tri_or_all = jnp.concatenate([tri, jnp.full((tq, 2 * tq), True)], axis=1)
    subln = subln_ref[...] * (1.0 - lambda_init)
    for hh in range(HEADS_PER_STEP):
        vt_ref[hh] = v_ref[0, :, hh * LANES:(hh + 1) * LANES].astype(_F32).T.astype(_BF16)

    def scores(n, hh, j):
        start = 2 * j * tq
        mid = start + tq
        cols = slice(hh * LANES, (hh + 1) * LANES)
        q = q_ref[0, start:start + 2 * tq, cols].astype(_F32)
        qq = jnp.concatenate([q[:tq] * keep_first, q[:tq] * keep_second,
                              q[tq:] * keep_first, q[tq:] * keep_second], axis=0).astype(_BF16)
        s_ref[n % 2, 0:mid, :] = _dot_nt(k_ref[0, 0:mid, cols], qq)
        s_ref[n % 2, mid:mid + tq, 2 * tq:4 * tq] = _dot_nt(k_ref[0, mid:mid + tq, cols], qq[2 * tq:])

    def finish(n, hh, j):
        slot = n % 2
        start = 2 * j * tq
        mid = start + tq
        cols = slice(hh * LANES, (hh + 1) * LANES)
        s_a = jnp.where(tri_or_all, s_ref[slot, start:mid, :], -1e30)
        s_b = jnp.where(tri, s_ref[slot, mid:mid + tq, 2 * tq:4 * tq], -1e30)
        m = jnp.max(s_a, axis=0, keepdims=True)
        if start > 0:
            m = jnp.maximum(m, jnp.max(s_ref[slot, 0:start, :], axis=0, keepdims=True))
        m_hi = jnp.maximum(m[:, 2 * tq:], jnp.max(s_b, axis=0, keepdims=True))
        m = jnp.concatenate([m[:, :2 * tq], m_hi], axis=1)
        p_a = jnp.exp2(s_a - m)
        l = jnp.sum(p_a, axis=0, keepdims=True)
        if start > 0:
            p_o = jnp.exp2(s_ref[slot, 0:start, :] - m)
            l = l + jnp.sum(p_o, axis=0, keepdims=True)
        p_b = jnp.exp2(s_b - m_hi)
        l_lo = l[:, :2 * tq]
        l_hi = l[:, 2 * tq:] + jnp.sum(p_b, axis=0, keepdims=True)
        ratio_lo = lam_full * l_lo[:, :tq] / l_lo[:, tq:]
        ratio_hi = lam_full * l_hi[:, :tq] / l_hi[:, tq:]

        def combine(p):
            return jnp.concatenate([p[:, 0:tq] - p[:, tq:2 * tq] * ratio_lo,
                                    p[:, 2 * tq:3 * tq] - p[:, 3 * tq:] * ratio_hi], axis=1).astype(_BF16)

        o_t = _dot(vt_ref[hh, :, start:mid], combine(p_a))
        if start > 0:
            o_t = o_t + _dot(vt_ref[hh, :, 0:start], combine(p_o))
        o_b = _dot(vt_ref[hh, :, mid:mid + tq], (p_b[:, :tq] - p_b[:, tq:] * ratio_hi).astype(_BF16))
        o_t = jnp.concatenate([o_t[:, :tq], o_t[:, tq:] + o_b], axis=1)
        inv = jnp.concatenate([1.0 / l_lo[:, :tq], 1.0 / l_hi[:, :tq]], axis=1)
        o = (o_t * inv).T
        o = o * lax.rsqrt(jnp.mean(o * o, axis=1, keepdims=True) + SUBLN_EPS) * subln
        o_ref[0, start:start + 2 * tq, cols] = o.astype(_BF16)

    work = [(hh, j) for hh in range(HEADS_PER_STEP) for j in range(seq // (2 * tq))]
    scores(0, *work[0])
    for n, item in enumerate(work):
        if n + 1 < len(work):
            scores(n + 1, *work[n + 1])
        finish(n, *item)


def _attention(qkv, lam_stack, subln, j, lambda_init, batch, seq, d_model):
    n_steps = d_model // (2 * HEAD_DIM) // HEADS_PER_STEP
    width = HEADS_PER_STEP * LANES
    qkv = qkv.reshape(batch, seq, 3 * d_model)
    blk = lambda off: pl.BlockSpec((1, seq, width), lambda b, h: (b, 0, off + h))
    out = pl.pallas_call(
        functools.partial(_attn_kernel, seq=seq, lambda_init=lambda_init),
        grid=(batch, n_steps),
        in_specs=[blk(0), blk(n_steps), blk(2 * n_steps),
                  _layer(lam_stack, j), _resident((1, LANES))],
        out_specs=pl.BlockSpec((1, seq, width), lambda b, h: (b, 0, h)),
        out_shape=jax.ShapeDtypeStruct((batch, seq, d_model), _BF16),
        scratch_shapes=[pltpu.VMEM((2, seq, 4 * Q_TILE), _F32),
                        pltpu.VMEM((HEADS_PER_STEP, LANES, seq), _BF16)],
        compiler_params=_params(("parallel", "parallel")),
        name="diff_attn",
    )(qkv, qkv, qkv, lam_stack, subln.reshape(1, LANES))
    return out.reshape(batch * seq, d_model)


def _ffn_kernel(*refs, has_proj, final, d_ff):
    refs = list(refs)
    x_ref = refs.pop(0)
    if has_proj:
        o_ref = refs.pop(0)
        wo_ref = refs.pop(0)
    g_ref, wgu_ref, wd_ref = refs.pop(0), refs.pop(0), refs.pop(0)
    if final:
        gf_ref = refs.pop(0)
    out_ref, act_ref = refs

    x = x_ref[...]
    if has_proj:
        x = x + _dot(o_ref[...], wo_ref[...])
    h = _rms(x, g_ref[...], NORM_EPS).astype(_BF16)
    fc = FF_CHUNK
    for j in range(d_ff // fc):
        gate = _dot(h, wgu_ref[:, j * fc:(j + 1) * fc])
        up = _dot(h, wgu_ref[:, d_ff + j * fc:d_ff + (j + 1) * fc])
        act = gate * (1.0 / (1.0 + jnp.exp(-gate))) * up
        act_ref[:, j * fc:(j + 1) * fc] = act.astype(_BF16)
    y = x + _dot(act_ref[...], wd_ref[...])
    if final:
        y = _rms(y, gf_ref[...], NORM_EPS)
    out_ref[...] = y


def _ffn(x, g, wgu_stack, wd_stack, layer, proj=None, final_g=None):
    m, d = x.shape
    d_ff = wd_stack.shape[1]
    tm = TOKEN_TILE
    row = lambda n: pl.BlockSpec((tm, n), lambda i: (i, 0))
    args, specs = [x], [row(d)]
    if proj is not None:
        o, wo_stack, j = proj
        args += [o, wo_stack]
        specs += [row(o.shape[1]), _layer(wo_stack, j)]
    args += [g, wgu_stack, wd_stack]
    specs += [_resident((1, d)), _layer(wgu_stack, layer), _layer(wd_stack, layer)]
    if final_g is not None:
        args.append(final_g)
        specs.append(_resident((1, d)))
    return pl.pallas_call(
        functools.partial(_ffn_kernel, has_proj=proj is not None,
                          final=final_g is not None, d_ff=d_ff),
        grid=(m // tm,),
        in_specs=specs,
        out_specs=row(d),
        out_shape=jax.ShapeDtypeStruct((m, d), _F32),
        scratch_shapes=[pltpu.VMEM((tm, d_ff), _BF16)],
        compiler_params=_params(("parallel",)),
        name="ffn",
    )(*args)


def _gmlp_kernel(x_ref, g_ref, win_ref, bin_ref, lng_ref, lnb_ref, ws_ref, bs_ref, wout_ref,
                 out_ref, u_ref, v_ref, vn_ref, gated_ref, *, width):
    tm = x_ref.shape[0]
    nc = 512
    gw = width // GMLP_GROUPS
    trow = lax.broadcasted_iota(jnp.int32, (CHUNK, CHUNK), 0)
    tcol = lax.broadcasted_iota(jnp.int32, (CHUNK, CHUNK), 1)
    tril = tcol <= trow
    ws = [jnp.where(tril, ws_ref[gi], 0.0).astype(_BF16) for gi in range(GMLP_GROUPS)]
    low_half = lax.broadcasted_iota(jnp.int32, (1, LANES), 1) < (gw - LANES)

    x = x_ref[...]
    h = _rms(x, g_ref[...], NORM_EPS).astype(_BF16)

    def in_proj(col):
        z = _dot(h, win_ref[:, col:col + nc]) + bin_ref[:, col:col + nc]
        return 0.5 * z * (1.0 + lax.erf(z * (1.0 / math.sqrt(2.0))))

    for col in range(0, width, nc):
        v_ref[:, col:col + nc] = in_proj(width + col)
    v = v_ref[...]
    mu = jnp.mean(v, axis=1, keepdims=True)
    vc = v - mu
    var = jnp.mean(vc * vc, axis=1, keepdims=True)
    vn_ref[...] = (vc * lax.rsqrt(var + LN_EPS) * lng_ref[...] + lnb_ref[...]).astype(_BF16)
    for col in range(0, width, nc):
        u_ref[:, col:col + nc] = in_proj(col)

    for c in range(tm // CHUNK):
        rows = slice(c * CHUNK, (c + 1) * CHUNK)
        for p in range(GMLP_GROUPS // 2):
            base = 2 * p * gw
            sa = _dot(ws[2 * p], vn_ref[rows, base:base + 2 * LANES])
            sb = _dot(ws[2 * p + 1], vn_ref[rows, base + LANES:base + 3 * LANES])
            pieces = (sa[:, :LANES],
                      jnp.where(low_half, sa[:, LANES:], sb[:, :LANES]),
                      sb[:, LANES:])
            for t, sv in enumerate(pieces):
                cols = slice(base + t * LANES, base + (t + 1) * LANES)
                gated_ref[rows, cols] = (u_ref[rows, cols] * (sv + bs_ref[:, cols])).astype(_BF16)
    out_ref[...] = x + _dot(gated_ref[...], wout_ref[...])


def _gmlp(x, g, w_in_stack, b_in, ln_g, ln_b, w_s, b_s, w_out_stack, j):
    m, d = x.shape
    width = w_out_stack.shape[1]
    tm = TOKEN_TILE
    gw = width // GMLP_GROUPS
    assert w_s.shape == (GMLP_GROUPS, CHUNK, CHUNK) and 2 * gw == 3 * LANES
    bias = jnp.broadcast_to(b_s.T[:, :, None], (CHUNK, GMLP_GROUPS, gw)).reshape(CHUNK, width)
    row = lambda n: pl.BlockSpec((tm, n), lambda i: (i, 0))
    return pl.pallas_call(
        functools.partial(_gmlp_kernel, width=width),
        grid=(m // tm,),
        in_specs=[row(d), _resident((1, d)), _layer(w_in_stack, j), _resident((1, 2 * width)),
                  _resident((1, width)), _resident((1, width)), _resident(w_s.shape),
                  _resident(bias.shape), _layer(w_out_stack, j)],
        out_specs=row(d),
        out_shape=jax.ShapeDtypeStruct((m, d), _F32),
        scratch_shapes=[pltpu.VMEM((tm, width), _F32), pltpu.VMEM((tm, width), _F32),
                        pltpu.VMEM((tm, width), _BF16), pltpu.VMEM((tm, width), _BF16)],
        compiler_params=_params(("parallel",)),
        name="gmlp",
    )(x, g, w_in_stack, b_in.reshape(1, -1), ln_g.reshape(1, -1), ln_b.reshape(1, -1), w_s, bias,
      w_out_stack)


def _conv_kernel(x_ref, g_ref, win_ref, cw_ref, wout_ref, out_ref, hc_ref, *, d_model, tiles_per_seq):
    tm = x_ref.shape[0]
    pad = 8

    @pl.when(pl.program_id(0) % tiles_per_seq == 0)
    def _():
        hc_ref[0:pad, :] = jnp.zeros((pad, d_model), _F32)

    x = x_ref[...]
    h = _rms(x, g_ref[...], NORM_EPS).astype(_BF16)
    gate_c = _dot(h, win_ref[:, d_model:2 * d_model])
    xs = _dot(h, win_ref[:, 2 * d_model:3 * d_model])
    hc = gate_c * xs
    hc_ref[pad:pad + tm, :] = hc
    conv = (cw_ref[0:1, :] * hc_ref[pad - 2:pad - 2 + tm, :]
            + cw_ref[1:2, :] * hc_ref[pad - 1:pad - 1 + tm, :]
            + cw_ref[2:3, :] * hc)
    hc_ref[0:pad, :] = hc_ref[tm:tm + pad, :]
    gate_b = _dot(h, win_ref[:, 0:d_model])
    out_ref[...] = x + _dot((gate_b * conv).astype(_BF16), wout_ref[...])


def _short_conv(x, g, w_in_stack, conv_w, w_out_stack, j, seq):
    m, d = x.shape
    tm = TOKEN_TILE
    row = lambda n: pl.BlockSpec((tm, n), lambda i: (i, 0))
    return pl.pallas_call(
        functools.partial(_conv_kernel, d_model=d, tiles_per_seq=seq // tm),
        grid=(m // tm,),
        in_specs=[row(d), _resident((1, d)), _layer(w_in_stack, j), _resident(conv_w.shape),
                  _layer(w_out_stack, j)],
        out_specs=row(d),
        out_shape=jax.ShapeDtypeStruct((m, d), _F32),
        scratch_shapes=[pltpu.VMEM((tm + 8, d), _F32)],
        compiler_params=_params(("arbitrary",)),
        name="short_conv",
    )(x, g, w_in_stack, conv_w, w_out_stack)


def kernel(x, positions, mix_norm, ffn_norm, final_norm, attn_w_in, attn_lambda, attn_subln, attn_w_out, gmlp_w_in, gmlp_b_in, gmlp_ln_g, gmlp_ln_b, gmlp_w_s, gmlp_b_s, gmlp_w_out, conv_w_in, conv_w, conv_w_out, ffn_w_gate_up, ffn_w_down):
    batch, seq, d = x.shape
    depth = mix_norm.shape[0]
    d_ff = ffn_w_down.shape[1]
    assert CONV_WIDTH == conv_w.shape[1] and seq % TOKEN_TILE == 0 and seq % (2 * Q_TILE) == 0
    assert d_ff % FF_CHUNK == 0 and TOKEN_TILE % CHUNK == 0
    assert (batch * seq) % (ROPE_ROWS * LANES // (HEAD_DIM // 2)) == 0
    assert d // (2 * HEAD_DIM) % HEADS_PER_STEP == 0

    xf = x.reshape(batch * seq, d)
    cos_t, sin_t = _rope_tables(positions)
    vec = lambda w: w.reshape(1, -1)
    attn_w_in, attn_w_out = attn_w_in.astype(_BF16), attn_w_out.astype(_BF16)
    gmlp_w_in, gmlp_w_out = gmlp_w_in.astype(_BF16), gmlp_w_out.astype(_BF16)
    conv_w_in, conv_w_out = conv_w_in.astype(_BF16), conv_w_out.astype(_BF16)
    ffn_w_gate_up, ffn_w_down = ffn_w_gate_up.astype(_BF16), ffn_w_down.astype(_BF16)

    for i in range(depth):
        kind, j = i % N_MIXERS, i // N_MIXERS
        proj = None
        if kind == 0:
            lambda_init = 0.8 - 0.6 * math.exp(-0.3 * i)
            qkv = _qkv_proj(xf, vec(mix_norm[i]), attn_w_in, j, cos_t, sin_t)
            o = _attention(qkv, attn_lambda, attn_subln[j], j, lambda_init, batch, seq, d)
            proj = (o, attn_w_out, j)
        elif kind == 1:
            xf = _gmlp(xf, vec(mix_norm[i]), gmlp_w_in, gmlp_b_in[j], gmlp_ln_g[j], gmlp_ln_b[j],
                       gmlp_w_s[j], gmlp_b_s[j], gmlp_w_out, j)
        else:
            xf = _short_conv(xf, vec(mix_norm[i]), conv_w_in, conv_w[j], conv_w_out, j, seq)
        xf = _ffn(xf, vec(ffn_norm[i]), ffn_w_gate_up, ffn_w_down, i, proj=proj,
                  final_g=vec(final_norm) if i == depth - 1 else None)
    return xf.reshape(batch, seq, d)
```

```python
import functools
import math

import jax
import jax.numpy as jnp
from jax import lax
from jax.experimental import pallas as pl
from jax.experimental.pallas import tpu as pltpu

N_MIXERS = 3
HEAD_DIM = 64
ROPE_THETA = 10000.0
GMLP_GROUPS = 8
CHUNK = 128
CONV_WIDTH = 3
NORM_EPS = 1e-6
SUBLN_EPS = 1e-5
LN_EPS = 1e-5

LANES = 128
TOKEN_TILE = 1024
Q_TILE = 256
HEADS_PER_STEP = 2
FF_CHUNK = 256
ROPE_ROWS = 2048
VMEM_LIMIT = 56 * 1024 * 1024

_BF16 = jnp.bfloat16
_F32 = jnp.float32


def _dot(a, b):
    return jnp.dot(a, b, preferred_element_type=_F32)


def _dot_nt(a, b):
    return lax.dot_general(a, b, (((1,), (1,)), ((), ())), preferred_element_type=_F32)


def _rms(x, g, eps):
    return x * lax.rsqrt(jnp.mean(x * x, axis=-1, keepdims=True) + eps) * g


def _resident(shape):
    return pl.BlockSpec(shape, lambda *_: (0,) * len(shape), pipeline_mode=pl.Buffered(1))


def _layer(stacked, j):
    _, rows, cols = stacked.shape
    return pl.BlockSpec((None, rows, cols), lambda *_: (j, 0, 0), pipeline_mode=pl.Buffered(1))


def _params(semantics):
    return pltpu.CompilerParams(dimension_semantics=semantics, vmem_limit_bytes=VMEM_LIMIT)


def _rope_table_kernel(pos_ref, invf_ref, cos_ref, sin_ref):
    half = HEAD_DIM // 2
    groups = LANES // half
    ang = pos_ref[...] * invf_ref[...]
    lane = lax.broadcasted_iota(jnp.int32, (1, LANES), 1)
    sign = jnp.where((lane % HEAD_DIM) < half, -1.0, 1.0)
    tables = ((jnp.cos(ang), cos_ref, None), (jnp.sin(ang), sin_ref, sign))
    for g in range(groups):
        keep = (lane // half == g).astype(_F32)
        for packed, out_ref, scale in tables:
            y = packed * keep
            y = y + pltpu.roll(y, half, 1)
            y = y + pltpu.roll(y, 2 * half, 1)
            out_ref[g] = y if scale is None else y * scale


def _rope_tables(positions):
    m = positions.size
    half = HEAD_DIM // 2
    groups = LANES // half
    rows = m // groups
    inv_freq = 1.0 / (ROPE_THETA ** (jnp.arange(0, HEAD_DIM, 2, dtype=_F32) / HEAD_DIM))
    pos = positions.astype(_F32).reshape(groups, rows).T
    pos = jnp.broadcast_to(pos[:, :, None], (rows, groups, half)).reshape(rows, LANES)
    invf = jnp.tile(inv_freq, groups).reshape(1, LANES)
    tile = ROPE_ROWS
    cos, sin = pl.pallas_call(
        _rope_table_kernel,
        grid=(rows // tile,),
        in_specs=[pl.BlockSpec((tile, LANES), lambda i: (i, 0)), _resident((1, LANES))],
        out_specs=[pl.BlockSpec((groups, tile, LANES), lambda i: (0, i, 0))] * 2,
        out_shape=[jax.ShapeDtypeStruct((groups, rows, LANES), _F32)] * 2,
        compiler_params=_params(("parallel",)),
        name="rope_tables",
    )(pos, invf)
    return cos.reshape(m, LANES), sin.reshape(m, LANES)


def _qkv_kernel(x_ref, g_ref, w_ref, cos_ref, sin_ref, out_ref, *, d_model):
    h = _rms(x_ref[...], g_ref[...], NORM_EPS).astype(_BF16)
    cos = cos_ref[...]
    sin = sin_ref[...]
    lane = lax.broadcasted_iota(jnp.int32, (1, LANES), 1)
    first_half = (lane % HEAD_DIM) < (HEAD_DIM // 2)
    for part in range(3):
        y = _dot(h, w_ref[:, part * d_model:(part + 1) * d_model])
        if part == 2:
            out_ref[:, part * d_model:(part + 1) * d_model] = y.astype(_BF16)
            continue
        if part == 0:
            y = y * (HEAD_DIM ** -0.5 * math.log2(math.e))
        for c in range(d_model // LANES):
            yc = y[:, c * LANES:(c + 1) * LANES]
            ahead = pltpu.roll(yc, LANES - HEAD_DIM // 2, 1)
            behind = pltpu.roll(yc, HEAD_DIM // 2, 1)
            rot = jnp.where(first_half, ahead, behind)
            col = part * d_model + c * LANES
            out_ref[:, col:col + LANES] = (yc * cos + rot * sin).astype(_BF16)


def _qkv_proj(x, g, w_stack, j, cos_t, sin_t):
    m, d = x.shape
    n = w_stack.shape[2]
    tm = TOKEN_TILE
    return pl.pallas_call(
        functools.partial(_qkv_kernel, d_model=d),
        grid=(m // tm,),
        in_specs=[
            pl.BlockSpec((tm, d), lambda i: (i, 0)),
            _resident((1, d)),
            _layer(w_stack, j),
            pl.BlockSpec((tm, LANES), lambda i: (i, 0)),
            pl.BlockSpec((tm, LANES), lambda i: (i, 0)),
        ],
        out_specs=pl.BlockSpec((tm, n), lambda i: (i, 0)),
        out_shape=jax.ShapeDtypeStruct((m, n), _BF16),
        compiler_params=_params(("parallel",)),
        name="attn_qkv",
    )(x, g, w_stack, cos_t, sin_t)


def _attn_kernel(q_ref, k_ref, v_ref, lam_ref, subln_ref, o_ref, s_ref, vt_ref, *, seq, lambda_init):
    tq = Q_TILE
    lam = lam_ref[...]
    lam_a = jnp.sum(jnp.sum(lam[0:1] * lam[1:2], axis=1, keepdims=True), axis=0, keepdims=True)
    lam_b = jnp.sum(jnp.sum(lam[2:3] * lam[3:4], axis=1, keepdims=True), axis=0, keepdims=True)
    lam_full = jnp.exp(lam_a) - jnp.exp(lam_b) + lambda_init

    lane = lax.broadcasted_iota(jnp.int32, (1, LANES), 1)
    keep_first = (lane < HEAD_DIM).astype(_F32)
    keep_second = 1.0 - keep_first
    key = lax.broadcasted_iota(jnp.int32, (tq, 2 * tq), 0)
    qry = lax.broadcasted_iota(jnp.int32, (tq, 2 * tq), 1)
    tri = key <= jnp.where(qry >= tq, qry - tq, qry)
    tri_or_all = jnp.concatenate([tri, jnp.full((tq, 2 * tq), True)], axis=1)
    subln = subln_ref[...] * (1.0 - lambda_init)
    for hh in range(HEADS_PER_STEP):
        vt_ref[hh] = v_ref[0, :, hh * LANES:(hh + 1) * LANES].astype(_F32).T.astype(_BF16)

    def scores(n, hh, j):
        start = 2 * j * tq
        mid = start + tq
        cols = slice(hh * LANES, (hh + 1) * LANES)
        q = q_ref[0, start:start + 2 * tq, cols].astype(_F32)
        qq = jnp.concatenate([q[:tq] * keep_first, q[:tq] * keep_second,
                              q[tq:] * keep_first, q[tq:] * keep_second], axis=0).astype(_BF16)
        s_ref[n % 2, 0:mid, :] = _dot_nt(k_ref[0, 0:mid, cols], qq)
        s_ref[n % 2, mid:mid + tq, 2 * tq:4 * tq] = _dot_nt(k_ref[0, mid:mid + tq, cols], qq[2 * tq:])

    def finish(n, hh, j):
        slot = n % 2
        start = 2 * j * tq
        mid = start + tq
        cols = slice(hh * LANES, (hh + 1) * LANES)
        s_a = jnp.where(tri_or_all, s_ref[slot, start:mid, :], -1e30)
        s_b = jnp.where(tri, s_ref[slot, mid:mid + tq, 2 * tq:4 * tq], -1e30)
        m = jnp.max(s_a, axis=0, keepdims=True)
        if start > 0:
            m = jnp.maximum(m, jnp.max(s_ref[slot, 0:start, :], axis=0, keepdims=True))
        m_hi = jnp.maximum(m[:, 2 * tq:], jnp.max(s_b, axis=0, keepdims=True))
        m = jnp.concatenate([m[:, :2 * tq], m_hi], axis=1)
        p_a = jnp.exp2(s_a - m)
        l = jnp.sum(p_a, axis=0, keepdims=True)
        if start > 0:
            p_o = jnp.exp2(s_ref[slot, 0:start, :] - m)
            l = l + jnp.sum(p_o, axis=0, keepdims=True)
        p_b = jnp.exp2(s_b - m_hi)
        l_lo = l[:, :2 * tq]
        l_hi = l[:, 2 * tq:] + jnp.sum(p_b, axis=0, keepdims=True)
        ratio_lo = lam_full * l_lo[:, :tq] / l_lo[:, tq:]
        ratio_hi = lam_full * l_hi[:, :tq] / l_hi[:, tq:]

        def combine(p):
            return jnp.concatenate([p[:, 0:tq] - p[:, tq:2 * tq] * ratio_lo,
                                    p[:, 2 * tq:3 * tq] - p[:, 3 * tq:] * ratio_hi], axis=1).astype(_BF16)

        o_t = _dot(vt_ref[hh, :, start:mid], combine(p_a))
        if start > 0:
            o_t = o_t + _dot(vt_ref[hh, :, 0:start], combine(p_o))
        o_b = _dot(vt_ref[hh, :, mid:mid + tq], (p_b[:, :tq] - p_b[:, tq:] * ratio_hi).astype(_BF16))
        o_t = jnp.concatenate([o_t[:, :tq], o_t[:, tq:] + o_b], axis=1)
        inv = jnp.concatenate([1.0 / l_lo[:, :tq], 1.0 / l_hi[:, :tq]], axis=1)
        o = (o_t * inv).T
        o = o * lax.rsqrt(jnp.mean(o * o, axis=1, keepdims=True) + SUBLN_EPS) * subln
        o_ref[0, start:start + 2 * tq, cols] = o.astype(_BF16)

    work = [(hh, j) for j in range(seq // (2 * tq)) for hh in range(HEADS_PER_STEP)]
    scores(0, *work[0])
    for n, item in enumerate(work):
        if n + 1 < len(work):
            scores(n + 1, *work[n + 1])
        finish(n, *item)


def _attention(qkv, lam_stack, subln, j, lambda_init, batch, seq, d_model):
    n_steps = d_model // (2 * HEAD_DIM) // HEADS_PER_STEP
    width = HEADS_PER_STEP * LANES
    qkv = qkv.reshape(batch, seq, 3 * d_model)
    blk = lambda off: pl.BlockSpec((1, seq, width), lambda b, h: (b, 0, off + h))
    out = pl.pallas_call(
        functools.partial(_attn_kernel, seq=seq, lambda_init=lambda_init),
        grid=(batch, n_steps),
        in_specs=[blk(0), blk(n_steps), blk(2 * n_steps),
                  _layer(lam_stack, j), _resident((1, LANES))],
        out_specs=pl.BlockSpec((1, seq, width), lambda b, h: (b, 0, h)),
        out_shape=jax.ShapeDtypeStruct((batch, seq, d_model), _BF16),
        scratch_shapes=[pltpu.VMEM((2, seq, 4 * Q_TILE), _F32),
                        pltpu.VMEM((HEADS_PER_STEP, LANES, seq), _BF16)],
        compiler_params=_params(("parallel", "parallel")),
        name="diff_attn",
    )(qkv, qkv, qkv, lam_stack, subln.reshape(1, LANES))
    return out.reshape(batch * seq, d_model)


def _ffn_kernel(*refs, has_proj, final, d_ff):
    refs = list(refs)
    x_ref = refs.pop(0)
    if has_proj:
        o_ref = refs.pop(0)
        wo_ref = refs.pop(0)
    g_ref, wgu_ref, wd_ref = refs.pop(0), refs.pop(0), refs.pop(0)
    if final:
        gf_ref = refs.pop(0)
    out_ref, act_ref = refs

    x = x_ref[...]
    if has_proj:
        x = x + _dot(o_ref[...], wo_ref[...])
    h = _rms(x, g_ref[...], NORM_EPS).astype(_BF16)
    fc = FF_CHUNK
    for j in range(d_ff // fc):
        gate = _dot(h, wgu_ref[:, j * fc:(j + 1) * fc])
        up = _dot(h, wgu_ref[:, d_ff + j * fc:d_ff + (j + 1) * fc])
        act = gate * (1.0 / (1.0 + jnp.exp(-gate))) * up
        act_ref[:, j * fc:(j + 1) * fc] = act.astype(_BF16)
    y = x + _dot(act_ref[...], wd_ref[...])
    if final:
        y = _rms(y, gf_ref[...], NORM_EPS)
    out_ref[...] = y


def _ffn(x, g, wgu_stack, wd_stack, layer, proj=None, final_g=None):
    m, d = x.shape
    d_ff = wd_stack.shape[1]
    tm = TOKEN_TILE
    row = lambda n: pl.BlockSpec((tm, n), lambda i: (i, 0))
    args, specs = [x], [row(d)]
    if proj is not None:
        o, wo_stack, j = proj
        args += [o, wo_stack]
        specs += [row(o.shape[1]), _layer(wo_stack, j)]
    args += [g, wgu_stack, wd_stack]
    specs += [_resident((1, d)), _layer(wgu_stack, layer), _layer(wd_stack, layer)]
    if final_g is not None:
        args.append(final_g)
        specs.append(_resident((1, d)))
    return pl.pallas_call(
        functools.partial(_ffn_kernel, has_proj=proj is not None,
                          final=final_g is not None, d_ff=d_ff),
        grid=(m // tm,),
        in_specs=specs,
        out_specs=row(d),
        out_shape=jax.ShapeDtypeStruct((m, d), _F32),
        scratch_shapes=[pltpu.VMEM((tm, d_ff), _BF16)],
        compiler_params=_params(("parallel",)),
        name="ffn",
    )(*args)


def _gmlp_kernel(x_ref, g_ref, win_ref, bin_ref, lng_ref, lnb_ref, ws_ref, bs_ref, wout_ref,
                 out_ref, u_ref, v_ref, vn_ref, gated_ref, *, width):
    tm = x_ref.shape[0]
    nc = 512
    gw = width // GMLP_GROUPS
    trow = lax.broadcasted_iota(jnp.int32, (CHUNK, CHUNK), 0)
    tcol = lax.broadcasted_iota(jnp.int32, (CHUNK, CHUNK), 1)
    tril = tcol <= trow
    ws = [jnp.where(tril, ws_ref[gi], 0.0).astype(_BF16) for gi in range(GMLP_GROUPS)]
    low_half = lax.broadcasted_iota(jnp.int32, (1, LANES), 1) < (gw - LANES)

    x = x_ref[...]
    h = _rms(x, g_ref[...], NORM_EPS).astype(_BF16)

    def in_proj(col):
        z = _dot(h, win_ref[:, col:col + nc]) + bin_ref[:, col:col + nc]
        return 0.5 * z * (1.0 + lax.erf(z * (1.0 / math.sqrt(2.0))))

    for col in range(0, width, nc):
        v_ref[:, col:col + nc] = in_proj(width + col)
    v = v_ref[...]
    mu = jnp.mean(v, axis=1, keepdims=True)
    vc = v - mu
    var = jnp.mean(vc * vc, axis=1, keepdims=True)
    vn_ref[...] = (vc * lax.rsqrt(var + LN_EPS) * lng_ref[...] + lnb_ref[...]).astype(_BF16)
    for col in range(0, width, nc):
        u_ref[:, col:col + nc] = in_proj(col)

    for c in range(tm // CHUNK):
        rows = slice(c * CHUNK, (c + 1) * CHUNK)
        for p in range(GMLP_GROUPS // 2):
            base = 2 * p * gw
            sa = _dot(ws[2 * p], vn_ref[rows, base:base + 2 * LANES])
            sb = _dot(ws[2 * p + 1], vn_ref[rows, base + LANES:base + 3 * LANES])
            pieces = (sa[:, :LANES],
                      jnp.where(low_half, sa[:, LANES:], sb[:, :LANES]),
                      sb[:, LANES:])
            for t, sv in enumerate(pieces):
                cols = slice(base + t * LANES, base + (t + 1) * LANES)
                gated_ref[rows, cols] = (u_ref[rows, cols] * (sv + bs_ref[:, cols])).astype(_BF16)
    out_ref[...] = x + _dot(gated_ref[...], wout_ref[...])


def _gmlp(x, g, w_in_stack, b_in, ln_g, ln_b, w_s, b_s, w_out_stack, j):
    m, d = x.shape
    width = w_out_stack.shape[1]
    tm = TOKEN_TILE
    gw = width // GMLP_GROUPS
    assert w_s.shape == (GMLP_GROUPS, CHUNK, CHUNK) and 2 * gw == 3 * LANES
    bias = jnp.broadcast_to(b_s.T[:, :, None], (CHUNK, GMLP_GROUPS, gw)).reshape(CHUNK, width)
    row = lambda n: pl.BlockSpec((tm, n), lambda i: (i, 0))
    return pl.pallas_call(
        functools.partial(_gmlp_kernel, width=width),
        grid=(m // tm,),
        in_specs=[row(d), _resident((1, d)), _layer(w_in_stack, j), _resident((1, 2 * width)),
                  _resident((1, width)), _resident((1, width)), _resident(w_s.shape),
                  _resident(bias.shape), _layer(w_out_stack, j)],
        out_specs=row(d),
        out_shape=jax.ShapeDtypeStruct((m, d), _F32),
        scratch_shapes=[pltpu.VMEM((tm, width), _F32), pltpu.VMEM((tm, width), _F32),
                        pltpu.VMEM((tm, width), _BF16), pltpu.VMEM((tm, width), _BF16)],
        compiler_params=_params(("parallel",)),
        name="gmlp",
    )(x, g, w_in_stack, b_in.reshape(1, -1), ln_g.reshape(1, -1), ln_b.reshape(1, -1), w_s, bias,
      w_out_stack)


def _conv_kernel(x_ref, g_ref, win_ref, cw_ref, wout_ref, out_ref, hc_ref, *, d_model, tiles_per_seq):
    tm = x_ref.shape[0]
    pad = 8

    @pl.when(pl.program_id(0) % tiles_per_seq == 0)
    def _():
        hc_ref[0:pad, :] = jnp.zeros((pad, d_model), _F32)

    x = x_ref[...]
    h = _rms(x, g_ref[...], NORM_EPS).astype(_BF16)
    gate_c = _dot(h, win_ref[:, d_model:2 * d_model])
    xs = _dot(h, win_ref[:, 2 * d_model:3 * d_model])
    hc = gate_c * xs
    hc_ref[pad:pad + tm, :] = hc
    conv = (cw_ref[0:1, :] * hc_ref[pad - 2:pad - 2 + tm, :]
            + cw_ref[1:2, :] * hc_ref[pad - 1:pad - 1 + tm, :]
            + cw_ref[2:3, :] * hc)
    hc_ref[0:pad, :] = hc_ref[tm:tm + pad, :]
    gate_b = _dot(h, win_ref[:, 0:d_model])
    out_ref[...] = x + _dot((gate_b * conv).astype(_BF16), wout_ref[...])


def _short_conv(x, g, w_in_stack, conv_w, w_out_stack, j, seq):
    m, d = x.shape
    tm = TOKEN_TILE
    row = lambda n: pl.BlockSpec((tm, n), lambda i: (i, 0))
    return pl.pallas_call(
        functools.partial(_conv_kernel, d_model=d, tiles_per_seq=seq // tm),
        grid=(m // tm,),
        in_specs=[row(d), _resident((1, d)), _layer(w_in_stack, j), _resident(conv_w.shape),
                  _layer(w_out_stack, j)],
        out_specs=row(d),
        out_shape=jax.ShapeDtypeStruct((m, d), _F32),
        scratch_shapes=[pltpu.VMEM((tm + 8, d), _F32)],
        compiler_params=_params(("arbitrary",)),
        name="short_conv",
    )(x, g, w_in_stack, conv_w, w_out_stack)


def kernel(x, positions, mix_norm, ffn_norm, final_norm, attn_w_in, attn_lambda, attn_subln, attn_w_out, gmlp_w_in, gmlp_b_in, gmlp_ln_g, gmlp_ln_b, gmlp_w_s, gmlp_b_s, gmlp_w_out, conv_w_in, conv_w, conv_w_out, ffn_w_gate_up, ffn_w_down):
    batch, seq, d = x.shape
    depth = mix_norm.shape[0]
    d_ff = ffn_w_down.shape[1]
    assert CONV_WIDTH == conv_w.shape[1] and seq % TOKEN_TILE == 0 and seq % (2 * Q_TILE) == 0
    assert d_ff % FF_CHUNK == 0 and TOKEN_TILE % CHUNK == 0
    assert (batch * seq) % (ROPE_ROWS * LANES // (HEAD_DIM // 2)) == 0
    assert d // (2 * HEAD_DIM) % HEADS_PER_STEP == 0

    xf = x.reshape(batch * seq, d)
    cos_t, sin_t = _rope_tables(positions)
    vec = lambda w: w.reshape(1, -1)
    attn_w_in, attn_w_out = attn_w_in.astype(_BF16), attn_w_out.astype(_BF16)
    gmlp_w_in, gmlp_w_out = gmlp_w_in.astype(_BF16), gmlp_w_out.astype(_BF16)
    conv_w_in, conv_w_out = conv_w_in.astype(_BF16), conv_w_out.astype(_BF16)
    ffn_w_gate_up, ffn_w_down = ffn_w_gate_up.astype(_BF16), ffn_w_down.astype(_BF16)

    for i in range(depth):
        kind, j = i % N_MIXERS, i // N_MIXERS
        proj = None
        if kind == 0:
            lambda_init = 0.8 - 0.6 * math.exp(-0.3 * i)
            qkv = _qkv_proj(xf, vec(mix_norm[i]), attn_w_in, j, cos_t, sin_t)
            o = _attention(qkv, attn_lambda, attn_subln[j], j, lambda_init, batch, seq, d)
            proj = (o, attn_w_out, j)
        elif kind == 1:
            xf = _gmlp(xf, vec(mix_norm[i]), gmlp_w_in, gmlp_b_in[j], gmlp_ln_g[j], gmlp_ln_b[j],
                       gmlp_w_s[j], gmlp_b_s[j], gmlp_w_out, j)
        else:
            xf = _short_conv(xf, vec(mix_norm[i]), conv_w_in, conv_w[j], conv_w_out, j, seq)
        xf = _ffn(xf, vec(ffn_norm[i]), ffn_w_gate_up, ffn_w_down, i, proj=proj,
                  final_g=vec(final_norm) if i == depth - 1 else None)
    return xf.reshape(batch, seq, d)
```

```python
import functools
import math

import jax
import jax.numpy as jnp
from jax import lax
from jax.experimental import pallas as pl
from jax.experimental.pallas import tpu as pltpu

N_MIXERS = 3
HEAD_DIM = 64
ROPE_THETA = 10000.0
GMLP_GROUPS = 8
CHUNK = 128
CONV_WIDTH = 3
NORM_EPS = 1e-6
SUBLN_EPS = 1e-5
LN_EPS = 1e-5

LANES = 128
TOKEN_TILE = 1024
Q_TILE = 256
HEADS_PER_STEP = 2
FF_CHUNK = 256
ROPE_ROWS = 2048
VMEM_LIMIT = 56 * 1024 * 1024

_BF16 = jnp.bfloat16
_F32 = jnp.float32


def _dot(a, b):
    return jnp.dot(a, b, preferred_element_type=_F32)


def _dot_nt(a, b):
    return lax.dot_general(a, b, (((1,), (1,)), ((), ())), preferred_element_type=_F32)


def _rms(x, g, eps):
    return x * lax.rsqrt(jnp.mean(x * x, axis=-1, keepdims=True) + eps) * g


def _resident(shape):
    return pl.BlockSpec(shape, lambda *_: (0,) * len(shape), pipeline_mode=pl.Buffered(1))


def _layer(stacked, j):
    _, rows, cols = stacked.shape
    return pl.BlockSpec((None, rows, cols), lambda *_: (j, 0, 0), pipeline_mode=pl.Buffered(1))


def _params(semantics):
    return pltpu.CompilerParams(dimension_semantics=semantics, vmem_limit_bytes=VMEM_LIMIT)


def _rope_table_kernel(pos_ref, invf_ref, cos_ref, sin_ref):
    half = HEAD_DIM // 2
    groups = LANES // half
    ang = pos_ref[...] * invf_ref[...]
    lane = lax.broadcasted_iota(jnp.int32, (1, LANES), 1)
    sign = jnp.where((lane % HEAD_DIM) < half, -1.0, 1.0)
    tables = ((jnp.cos(ang), cos_ref, None), (jnp.sin(ang), sin_ref, sign))
    for g in range(groups):
        keep = (lane // half == g).astype(_F32)
        for packed, out_ref, scale in tables:
            y = packed * keep
            y = y + pltpu.roll(y, half, 1)
            y = y + pltpu.roll(y, 2 * half, 1)
            out_ref[g] = y if scale is None else y * scale


def _rope_tables(positions):
    m = positions.size
    half = HEAD_DIM // 2
    groups = LANES // half
    rows = m // groups
    inv_freq = 1.0 / (ROPE_THETA ** (jnp.arange(0, HEAD_DIM, 2, dtype=_F32) / HEAD_DIM))
    pos = positions.astype(_F32).reshape(groups, rows).T
    pos = jnp.broadcast_to(pos[:, :, None], (rows, groups, half)).reshape(rows, LANES)
    invf = jnp.tile(inv_freq, groups).reshape(1, LANES)
    tile = ROPE_ROWS
    cos, sin = pl.pallas_call(
        _rope_table_kernel,
        grid=(rows // tile,),
        in_specs=[pl.BlockSpec((tile, LANES), lambda i: (i, 0)), _resident((1, LANES))],
        out_specs=[pl.BlockSpec((groups, tile, LANES), lambda i: (0, i, 0))] * 2,
        out_shape=[jax.ShapeDtypeStruct((groups, rows, LANES), _F32)] * 2,
        compiler_params=_params(("parallel",)),
        name="rope_tables",
    )(pos, invf)
    return cos.reshape(m, LANES), sin.reshape(m, LANES)


def _qkv_kernel(x_ref, g_ref, w_ref, cos_ref, sin_ref, out_ref, *, d_model):
    h = _rms(x_ref[...], g_ref[...], NORM_EPS).astype(_BF16)
    cos = cos_ref[...]
    sin = sin_ref[...]
    lane = lax.broadcasted_iota(jnp.int32, (1, LANES), 1)
    first_half = (lane % HEAD_DIM) < (HEAD_DIM // 2)
    first_map = lane < HEAD_DIM
    for part in range(3):
        y = _dot(h, w_ref[:, part * d_model:(part + 1) * d_model])
        if part == 2:
            out_ref[:, 3 * d_model:4 * d_model] = y.astype(_BF16)
            continue
        if part == 0:
            y = y * (HEAD_DIM ** -0.5 * math.log2(math.e))
        for c in range(d_model // LANES):
            yc = y[:, c * LANES:(c + 1) * LANES]
            ahead = pltpu.roll(yc, LANES - HEAD_DIM // 2, 1)
            behind = pltpu.roll(yc, HEAD_DIM // 2, 1)
            rot = jnp.where(first_half, ahead, behind)
            roped = yc * cos + rot * sin
            col = c * LANES
            if part == 0:
                out_ref[:, col:col + LANES] = jnp.where(first_map, roped, 0.0).astype(_BF16)
                out_ref[:, d_model + col:d_model + col + LANES] = jnp.where(first_map, 0.0, roped).astype(_BF16)
            else:
                out_ref[:, 2 * d_model + col:2 * d_model + col + LANES] = roped.astype(_BF16)


def _qkv_proj(x, g, w_stack, j, cos_t, sin_t):
    m, d = x.shape
    n = 4 * d
    tm = TOKEN_TILE
    return pl.pallas_call(
        functools.partial(_qkv_kernel, d_model=d),
        grid=(m // tm,),
        in_specs=[
            pl.BlockSpec((tm, d), lambda i: (i, 0)),
            _resident((1, d)),
            _layer(w_stack, j),
            pl.BlockSpec((tm, LANES), lambda i: (i, 0)),
            pl.BlockSpec((tm, LANES), lambda i: (i, 0)),
        ],
        out_specs=pl.BlockSpec((tm, n), lambda i: (i, 0)),
        out_shape=jax.ShapeDtypeStruct((m, n), _BF16),
        compiler_params=_params(("parallel",)),
        name="attn_qkv",
    )(x, g, w_stack, cos_t, sin_t)


def _attn_kernel(q1_ref, q2_ref, k_ref, v_ref, lam_ref, subln_ref, o_ref, s_ref, vt_ref, *, seq, lambda_init):
    tq = Q_TILE
    lam = lam_ref[...]
    lam_a = jnp.sum(jnp.sum(lam[0:1] * lam[1:2], axis=1, keepdims=True), axis=0, keepdims=True)
    lam_b = jnp.sum(jnp.sum(lam[2:3] * lam[3:4], axis=1, keepdims=True), axis=0, keepdims=True)
    lam_full = jnp.exp(lam_a) - jnp.exp(lam_b) + lambda_init

    key = lax.broadcasted_iota(jnp.int32, (tq, 2 * tq), 0)
    qry = lax.broadcasted_iota(jnp.int32, (tq, 2 * tq), 1)
    tri = key <= jnp.where(qry >= tq, qry - tq, qry)
    tri_or_all = jnp.concatenate([tri, jnp.full((tq, 2 * tq), True)], axis=1)
    subln = subln_ref[...] * (1.0 - lambda_init)
    for hh in range(HEADS_PER_STEP):
        vt_ref[hh] = v_ref[0, :, hh * LANES:(hh + 1) * LANES].astype(_F32).T.astype(_BF16)

    def scores(n, hh, j):
        start = 2 * j * tq
        mid = start + tq
        cols = slice(hh * LANES, (hh + 1) * LANES)
        qq = jnp.concatenate([q1_ref[0, start:mid, cols], q2_ref[0, start:mid, cols],
                              q1_ref[0, mid:mid + tq, cols], q2_ref[0, mid:mid + tq, cols]], axis=0)
        s_ref[n % 2, 0:mid, :] = _dot_nt(k_ref[0, 0:mid, cols], qq)
        s_ref[n % 2, mid:mid + tq, 2 * tq:4 * tq] = _dot_nt(k_ref[0, mid:mid + tq, cols], qq[2 * tq:])

    def finish(n, hh, j):
        slot = n % 2
        start = 2 * j * tq
        mid = start + tq
        cols = slice(hh * LANES, (hh + 1) * LANES)
        s_a = jnp.where(tri_or_all, s_ref[slot, start:mid, :], -1e30)
        s_b = jnp.where(tri, s_ref[slot, mid:mid + tq, 2 * tq:4 * tq], -1e30)
        m = jnp.max(s_a, axis=0, keepdims=True)
        if start > 0:
            m = jnp.maximum(m, jnp.max(s_ref[slot, 0:start, :], axis=0, keepdims=True))
        m_hi = jnp.maximum(m[:, 2 * tq:], jnp.max(s_b, axis=0, keepdims=True))
        m = jnp.concatenate([m[:, :2 * tq], m_hi], axis=1)
        p_a = jnp.exp2(s_a - m)
        l = jnp.sum(p_a, axis=0, keepdims=True)
        if start > 0:
            p_o = jnp.exp2(s_ref[slot, 0:start, :] - m)
            l = l + jnp.sum(p_o, axis=0, keepdims=True)
        p_b = jnp.exp2(s_b - m_hi)
        l_lo = l[:, :2 * tq]
        l_hi = l[:, 2 * tq:] + jnp.sum(p_b, axis=0, keepdims=True)
        ratio_lo = lam_full * l_lo[:, :tq] / l_lo[:, tq:]
        ratio_hi = lam_full * l_hi[:, :tq] / l_hi[:, tq:]

        def combine(p):
            return jnp.concatenate([p[:, 0:tq] - p[:, tq:2 * tq] * ratio_lo,
                                    p[:, 2 * tq:3 * tq] - p[:, 3 * tq:] * ratio_hi], axis=1).astype(_BF16)

        o_t = _dot(vt_ref[hh, :, start:mid], combine(p_a))
        if start > 0:
            o_t = o_t + _dot(vt_ref[hh, :, 0:start], combine(p_o))
        o_b = _dot(vt_ref[hh, :, mid:mid + tq], (p_b[:, :tq] - p_b[:, tq:] * ratio_hi).astype(_BF16))
        o_t = jnp.concatenate([o_t[:, :tq], o_t[:, tq:] + o_b], axis=1)
        inv = jnp.concatenate([1.0 / l_lo[:, :tq], 1.0 / l_hi[:, :tq]], axis=1)
        o = (o_t * inv).T
        o = o * lax.rsqrt(jnp.mean(o * o, axis=1, keepdims=True) + SUBLN_EPS) * subln
        o_ref[0, start:start + 2 * tq, cols] = o.astype(_BF16)

    work = [(hh, j) for j in range(seq // (2 * tq)) for hh in range(HEADS_PER_STEP)]
    scores(0, *work[0])
    for n, item in enumerate(work):
        if n + 1 < len(work):
            scores(n + 1, *work[n + 1])
        finish(n, *item)


def _attention(qkv, lam_stack, subln, j, lambda_init, batch, seq, d_model):
    n_steps = d_model // (2 * HEAD_DIM) // HEADS_PER_STEP
    width = HEADS_PER_STEP * LANES
    qkv = qkv.reshape(batch, seq, 4 * d_model)
    blk = lambda off: pl.BlockSpec((1, seq, width), lambda b, h: (b, 0, off + h))
    out = pl.pallas_call(
        functools.partial(_attn_kernel, seq=seq, lambda_init=lambda_init),
        grid=(batch, n_steps),
        in_specs=[blk(0), blk(n_steps), blk(2 * n_steps), blk(3 * n_steps),
                  _layer(lam_stack, j), _resident((1, LANES))],
        out_specs=pl.BlockSpec((1, seq, width), lambda b, h: (b, 0, h)),
        out_shape=jax.ShapeDtypeStruct((batch, seq, d_model), _BF16),
        scratch_shapes=[pltpu.VMEM((2, seq, 4 * Q_TILE), _F32),
                        pltpu.VMEM((HEADS_PER_STEP, LANES, seq), _BF16)],
        compiler_params=_params(("parallel", "parallel")),
        name="diff_attn",
    )(qkv, qkv, qkv, qkv, lam_stack, subln.reshape(1, LANES))
    return out.reshape(batch * seq, d_model)


def _ffn_kernel(*refs, has_proj, final, d_ff):
    refs = list(refs)
    x_ref = refs.pop(0)
    if has_proj:
        o_ref = refs.pop(0)
        wo_ref = refs.pop(0)
    g_ref, wgu_ref, wd_ref = refs.pop(0), refs.pop(0), refs.pop(0)
    if final:
        gf_ref = refs.pop(0)
    out_ref, act_ref = refs

    x = x_ref[...]
    if has_proj:
        x = x + _dot(o_ref[...], wo_ref[...])
    h = _rms(x, g_ref[...], NORM_EPS).astype(_BF16)
    fc = FF_CHUNK
    for j in range(d_ff // fc):
        gate = _dot(h, wgu_ref[:, j * fc:(j + 1) * fc])
        up = _dot(h, wgu_ref[:, d_ff + j * fc:d_ff + (j + 1) * fc])
        act = gate * (1.0 / (1.0 + jnp.exp(-gate))) * up
        act_ref[:, j * fc:(j + 1) * fc] = act.astype(_BF16)
    y = x + _dot(act_ref[...], wd_ref[...])
    if final:
        y = _rms(y, gf_ref[...], NORM_EPS)
    out_ref[...] = y


def _ffn(x, g, wgu_stack, wd_stack, layer, proj=None, final_g=None):
    m, d = x.shape
    d_ff = wd_stack.shape[1]
    tm = TOKEN_TILE
    row = lambda n: pl.BlockSpec((tm, n), lambda i: (i, 0))
    args, specs = [x], [row(d)]
    if proj is not None:
        o, wo_stack, j = proj
        args += [o, wo_stack]
        specs += [row(o.shape[1]), _layer(wo_stack, j)]
    args += [g, wgu_stack, wd_stack]
    specs += [_resident((1, d)), _layer(wgu_stack, layer), _layer(wd_stack, layer)]
    if final_g is not None:
        args.append(final_g)
        specs.append(_resident((1, d)))
    return pl.pallas_call(
        functools.partial(_ffn_kernel, has_proj=proj is not None,
                          final=final_g is not None, d_ff=d_ff),
        grid=(m // tm,),
        in_specs=specs,
        out_specs=row(d),
        out_shape=jax.ShapeDtypeStruct((m, d), _F32),
        scratch_shapes=[pltpu.VMEM((tm, d_ff), _BF16)],
        compiler_params=_params(("parallel",)),
        name="ffn",
    )(*args)


def _gmlp_kernel(x_ref, g_ref, win_ref, bin_ref, lng_ref, lnb_ref, ws_ref, bs_ref, wout_ref,
                 out_ref, u_ref, v_ref, vn_ref, gated_ref, *, width):
    tm = x_ref.shape[0]
    nc = 512
    gw = width // GMLP_GROUPS
    trow = lax.broadcasted_iota(jnp.int32, (CHUNK, CHUNK), 0)
    tcol = lax.broadcasted_iota(jnp.int32, (CHUNK, CHUNK), 1)
    tril = tcol <= trow
    ws = [jnp.where(tril, ws_ref[gi], 0.0).astype(_BF16) for gi in range(GMLP_GROUPS)]
    low_half = lax.broadcasted_iota(jnp.int32, (1, LANES), 1) < (gw - LANES)

    x = x_ref[...]
    h = _rms(x, g_ref[...], NORM_EPS).astype(_BF16)

    def in_proj(col):
        z = _dot(h, win_ref[:, col:col + nc]) + bin_ref[:, col:col + nc]
        return 0.5 * z * (1.0 + lax.erf(z * (1.0 / math.sqrt(2.0))))

    for col in range(0, width, nc):
        v_ref[:, col:col + nc] = in_proj(width + col)
    v = v_ref[...]
    mu = jnp.mean(v, axis=1, keepdims=True)
    vc = v - mu
    var = jnp.mean(vc * vc, axis=1, keepdims=True)
    vn_ref[...] = (vc * lax.rsqrt(var + LN_EPS) * lng_ref[...] + lnb_ref[...]).astype(_BF16)
    for col in range(0, width, nc):
        u_ref[:, col:col + nc] = in_proj(col)

    for c in range(tm // CHUNK):
        rows = slice(c * CHUNK, (c + 1) * CHUNK)
        for p in range(GMLP_GROUPS // 2):
            base = 2 * p * gw
            sa = _dot(ws[2 * p], vn_ref[rows, base:base + 2 * LANES])
            sb = _dot(ws[2 * p + 1], vn_ref[rows, base + LANES:base + 3 * LANES])
            pieces = (sa[:, :LANES],
                      jnp.where(low_half, sa[:, LANES:], sb[:, :LANES]),
                      sb[:, LANES:])
            for t, sv in enumerate(pieces):
                cols = slice(base + t * LANES, base + (t + 1) * LANES)
                gated_ref[rows, cols] = (u_ref[rows, cols] * (sv + bs_ref[:, cols])).astype(_BF16)
    out_ref[...] = x + _dot(gated_ref[...], wout_ref[...])


def _gmlp(x, g, w_in_stack, b_in, ln_g, ln_b, w_s, b_s, w_out_stack, j):
    m, d = x.shape
    width = w_out_stack.shape[1]
    tm = TOKEN_TILE
    gw = width // GMLP_GROUPS
    assert w_s.shape == (GMLP_GROUPS, CHUNK, CHUNK) and 2 * gw == 3 * LANES
    bias = jnp.broadcast_to(b_s.T[:, :, None], (CHUNK, GMLP_GROUPS, gw)).reshape(CHUNK, width)
    row = lambda n: pl.BlockSpec((tm, n), lambda i: (i, 0))
    return pl.pallas_call(
        functools.partial(_gmlp_kernel, width=width),
        grid=(m // tm,),
        in_specs=[row(d), _resident((1, d)), _layer(w_in_stack, j), _resident((1, 2 * width)),
                  _resident((1, width)), _resident((1, width)), _resident(w_s.shape),
                  _resident(bias.shape), _layer(w_out_stack, j)],
        out_specs=row(d),
        out_shape=jax.ShapeDtypeStruct((m, d), _F32),
        scratch_shapes=[pltpu.VMEM((tm, width), _F32), pltpu.VMEM((tm, width), _F32),
                        pltpu.VMEM((tm, width), _BF16), pltpu.VMEM((tm, width), _BF16)],
        compiler_params=_params(("parallel",)),
        name="gmlp",
    )(x, g, w_in_stack, b_in.reshape(1, -1), ln_g.reshape(1, -1), ln_b.reshape(1, -1), w_s, bias,
      w_out_stack)


def _conv_kernel(x_ref, g_ref, win_ref, cw_ref, wout_ref, out_ref, hc_ref, *, d_model, tiles_per_seq):
    tm = x_ref.shape[0]
    pad = 8

    @pl.when(pl.program_id(0) % tiles_per_seq == 0)
    def _():
        hc_ref[0:pad, :] = jnp.zeros((pad, d_model), _F32)

    x = x_ref[...]
    h = _rms(x, g_ref[...], NORM_EPS).astype(_BF16)
    gate_c = _dot(h, win_ref[:, d_model:2 * d_model])
    xs = _dot(h, win_ref[:, 2 * d_model:3 * d_model])
    hc = gate_c * xs
    hc_ref[pad:pad + tm, :] = hc
    conv = (cw_ref[0:1, :] * hc_ref[pad - 2:pad - 2 + tm, :]
            + cw_ref[1:2, :] * hc_ref[pad - 1:pad - 1 + tm, :]
            + cw_ref[2:3, :] * hc)
    hc_ref[0:pad, :] = hc_ref[tm:tm + pad, :]
    gate_b = _dot(h, win_ref[:, 0:d_model])
    out_ref[...] = x + _dot((gate_b * conv).astype(_BF16), wout_ref[...])


def _short_conv(x, g, w_in_stack, conv_w, w_out_stack, j, seq):
    m, d = x.shape
    tm = TOKEN_TILE
    row = lambda n: pl.BlockSpec((tm, n), lambda i: (i, 0))
    return pl.pallas_call(
        functools.partial(_conv_kernel, d_model=d, tiles_per_seq=seq // tm),
        grid=(m // tm,),
        in_specs=[row(d), _resident((1, d)), _layer(w_in_stack, j), _resident(conv_w.shape),
                  _layer(w_out_stack, j)],
        out_specs=row(d),
        out_shape=jax.ShapeDtypeStruct((m, d), _F32),
        scratch_shapes=[pltpu.VMEM((tm + 8, d), _F32)],
        compiler_params=_params(("arbitrary",)),
        name="short_conv",
    )(x, g, w_in_stack, conv_w, w_out_stack)


def kernel(x, positions, mix_norm, ffn_norm, final_norm, attn_w_in, attn_lambda, attn_subln, attn_w_out, gmlp_w_in, gmlp_b_in, gmlp_ln_g, gmlp_ln_b, gmlp_w_s, gmlp_b_s, gmlp_w_out, conv_w_in, conv_w, conv_w_out, ffn_w_gate_up, ffn_w_down):
    batch, seq, d = x.shape
    depth = mix_norm.shape[0]
    d_ff = ffn_w_down.shape[1]
    assert CONV_WIDTH == conv_w.shape[1] and seq % TOKEN_TILE == 0 and seq % (2 * Q_TILE) == 0
    assert d_ff % FF_CHUNK == 0 and TOKEN_TILE % CHUNK == 0
    assert (batch * seq) % (ROPE_ROWS * LANES // (HEAD_DIM // 2)) == 0
    assert d // (2 * HEAD_DIM) % HEADS_PER_STEP == 0

    xf = x.reshape(batch * seq, d)
    cos_t, sin_t = _rope_tables(positions)
    vec = lambda w: w.reshape(1, -1)
    attn_w_in, attn_w_out = attn_w_in.astype(_BF16), attn_w_out.astype(_BF16)
    gmlp_w_in, gmlp_w_out = gmlp_w_in.astype(_BF16), gmlp_w_out.astype(_BF16)
    conv_w_in, conv_w_out = conv_w_in.astype(_BF16), conv_w_out.astype(_BF16)
    ffn_w_gate_up, ffn_w_down = ffn_w_gate_up.astype(_BF16), ffn_w_down.astype(_BF16)

    for i in range(depth):
        kind, j = i % N_MIXERS, i // N_MIXERS
        proj = None
        if kind == 0:
            lambda_init = 0.8 - 0.6 * math.exp(-0.3 * i)
            qkv = _qkv_proj(xf, vec(mix_norm[i]), attn_w_in, j, cos_t, sin_t)
            o = _attention(qkv, attn_lambda, attn_subln[j], j, lambda_init, batch, seq, d)
            proj = (o, attn_w_out, j)
        elif kind == 1:
            xf = _gmlp(xf, vec(mix_norm[i]), gmlp_w_in, gmlp_b_in[j], gmlp_ln_g[j], gmlp_ln_b[j],
                       gmlp_w_s[j], gmlp_b_s[j], gmlp_w_out, j)
        else:
            xf = _short_conv(xf, vec(mix_norm[i]), conv_w_in, conv_w[j], conv_w_out, j, seq)
        xf = _ffn(xf, vec(ffn_norm[i]), ffn_w_gate_up, ffn_w_down, i, proj=proj,
                  final_g=vec(final_norm) if i == depth - 1 else None)
    return xf.reshape(batch, seq, d)
```

```python
import functools
import math

import jax
import jax.numpy as jnp
from jax import lax
from jax.experimental import pallas as pl
from jax.experimental.pallas import tpu as pltpu

N_MIXERS = 3
HEAD_DIM = 64
ROPE_THETA = 10000.0
GMLP_GROUPS = 8
CHUNK = 128
CONV_WIDTH = 3
NORM_EPS = 1e-6
SUBLN_EPS = 1e-5
LN_EPS = 1e-5

LANES = 128
TOKEN_TILE = 1024
Q_TILE = 256
HEADS_PER_STEP = 2
FF_CHUNK = 256
ROPE_ROWS = 2048
VMEM_LIMIT = 56 * 1024 * 1024

_BF16 = jnp.bfloat16
_F32 = jnp.float32


def _dot(a, b):
    return jnp.dot(a, b, preferred_element_type=_F32)


def _dot_nt(a, b):
    return lax.dot_general(a, b, (((1,), (1,)), ((), ())), preferred_element_type=_F32)


def _rms(x, g, eps):
    return x * lax.rsqrt(jnp.mean(x * x, axis=-1, keepdims=True) + eps) * g


def _resident(shape):
    return pl.BlockSpec(shape, lambda *_: (0,) * len(shape), pipeline_mode=pl.Buffered(1))


def _layer(stacked, j):
    _, rows, cols = stacked.shape
    return pl.BlockSpec((None, rows, cols), lambda *_: (j, 0, 0), pipeline_mode=pl.Buffered(1))


def _params(semantics):
    return pltpu.CompilerParams(dimension_semantics=semantics, vmem_limit_bytes=VMEM_LIMIT)


def _rope_table_kernel(pos_ref, invf_ref, cos_ref, sin_ref):
    half = HEAD_DIM // 2
    groups = LANES // half
    ang = pos_ref[...] * invf_ref[...]
    lane = lax.broadcasted_iota(jnp.int32, (1, LANES), 1)
    sign = jnp.where((lane % HEAD_DIM) < half, -1.0, 1.0)
    tables = ((jnp.cos(ang), cos_ref, None), (jnp.sin(ang), sin_ref, sign))
    for g in range(groups):
        keep = (lane // half == g).astype(_F32)
        for packed, out_ref, scale in tables:
            y = packed * keep
            y = y + pltpu.roll(y, half, 1)
            y = y + pltpu.roll(y, 2 * half, 1)
            out_ref[g] = y if scale is None else y * scale


def _rope_tables(positions):
    m = positions.size
    half = HEAD_DIM // 2
    groups = LANES // half
    rows = m // groups
    inv_freq = 1.0 / (ROPE_THETA ** (jnp.arange(0, HEAD_DIM, 2, dtype=_F32) / HEAD_DIM))
    pos = positions.astype(_F32).reshape(groups, rows).T
    pos = jnp.broadcast_to(pos[:, :, None], (rows, groups, half)).reshape(rows, LANES)
    invf = jnp.tile(inv_freq, groups).reshape(1, LANES)
    tile = ROPE_ROWS
    cos, sin = pl.pallas_call(
        _rope_table_kernel,
        grid=(rows // tile,),
        in_specs=[pl.BlockSpec((tile, LANES), lambda i: (i, 0)), _resident((1, LANES))],
        out_specs=[pl.BlockSpec((groups, tile, LANES), lambda i: (0, i, 0))] * 2,
        out_shape=[jax.ShapeDtypeStruct((groups, rows, LANES), _F32)] * 2,
        compiler_params=_params(("parallel",)),
        name="rope_tables",
    )(pos, invf)
    return cos.reshape(m, LANES), sin.reshape(m, LANES)


def _qkv_kernel(x_ref, g_ref, w_ref, cos_ref, sin_ref, out_ref, *, d_model):
    h = _rms(x_ref[...], g_ref[...], NORM_EPS).astype(_BF16)
    cos = cos_ref[...]
    sin = sin_ref[...]
    lane = lax.broadcasted_iota(jnp.int32, (1, LANES), 1)
    first_half = (lane % HEAD_DIM) < (HEAD_DIM // 2)
    first_map = lane < HEAD_DIM
    for part in range(3):
        y = _dot(h, w_ref[:, part * d_model:(part + 1) * d_model])
        if part == 2:
            out_ref[:, 3 * d_model:4 * d_model] = y.astype(_BF16)
            continue
        if part == 0:
            y = y * (HEAD_DIM ** -0.5 * math.log2(math.e))
        for c in range(d_model // LANES):
            yc = y[:, c * LANES:(c + 1) * LANES]
            ahead = pltpu.roll(yc, LANES - HEAD_DIM // 2, 1)
            behind = pltpu.roll(yc, HEAD_DIM // 2, 1)
            rot = jnp.where(first_half, ahead, behind)
            roped = yc * cos + rot * sin
            col = c * LANES
            if part == 0:
                out_ref[:, col:col + LANES] = jnp.where(first_map, roped, 0.0).astype(_BF16)
                out_ref[:, d_model + col:d_model + col + LANES] = jnp.where(first_map, 0.0, roped).astype(_BF16)
            else:
                out_ref[:, 2 * d_model + col:2 * d_model + col + LANES] = roped.astype(_BF16)


def _qkv_proj(x, g, w_stack, j, cos_t, sin_t):
    m, d = x.shape
    n = 4 * d
    tm = TOKEN_TILE
    return pl.pallas_call(
        functools.partial(_qkv_kernel, d_model=d),
        grid=(m // tm,),
        in_specs=[
            pl.BlockSpec((tm, d), lambda i: (i, 0)),
            _resident((1, d)),
            _layer(w_stack, j),
            pl.BlockSpec((tm, LANES), lambda i: (i, 0)),
            pl.BlockSpec((tm, LANES), lambda i: (i, 0)),
        ],
        out_specs=pl.BlockSpec((tm, n), lambda i: (i, 0)),
        out_shape=jax.ShapeDtypeStruct((m, n), _BF16),
        compiler_params=_params(("parallel",)),
        name="attn_qkv",
    )(x, g, w_stack, cos_t, sin_t)


def _attn_kernel(q1_ref, q2_ref, k_ref, v_ref, lam_ref, subln_ref, o_ref, s_ref, vt_ref, *, seq, lambda_init):
    tq = Q_TILE
    lam = lam_ref[...]
    lam_a = jnp.sum(jnp.sum(lam[0:1] * lam[1:2], axis=1, keepdims=True), axis=0, keepdims=True)
    lam_b = jnp.sum(jnp.sum(lam[2:3] * lam[3:4], axis=1, keepdims=True), axis=0, keepdims=True)
    lam_full = jnp.exp(lam_a) - jnp.exp(lam_b) + lambda_init

    key = lax.broadcasted_iota(jnp.int32, (tq, 2 * tq), 0)
    qry = lax.broadcasted_iota(jnp.int32, (tq, 2 * tq), 1)
    tri = key <= jnp.where(qry >= tq, qry - tq, qry)
    tri_or_all = jnp.concatenate([tri, jnp.full((tq, 2 * tq), True)], axis=1)
    subln = subln_ref[...] * (1.0 - lambda_init)
    for hh in range(HEADS_PER_STEP):
        vt_ref[hh] = v_ref[0, :, hh * LANES:(hh + 1) * LANES].T

    def scores(n, hh, j):
        start = 2 * j * tq
        mid = start + tq
        cols = slice(hh * LANES, (hh + 1) * LANES)
        qq = jnp.concatenate([q1_ref[0, start:mid, cols], q2_ref[0, start:mid, cols],
                              q1_ref[0, mid:mid + tq, cols], q2_ref[0, mid:mid + tq, cols]], axis=0)
        s_ref[n % 2, 0:mid, :] = _dot_nt(k_ref[0, 0:mid, cols], qq)
        s_ref[n % 2, mid:mid + tq, 2 * tq:4 * tq] = _dot_nt(k_ref[0, mid:mid + tq, cols], qq[2 * tq:])

    def finish(n, hh, j):
        slot = n % 2
        start = 2 * j * tq
        mid = start + tq
        cols = slice(hh * LANES, (hh + 1) * LANES)
        s_a = jnp.where(tri_or_all, s_ref[slot, start:mid, :], -1e30)
        s_b = jnp.where(tri, s_ref[slot, mid:mid + tq, 2 * tq:4 * tq], -1e30)
        m = jnp.max(s_a, axis=0, keepdims=True)
        if start > 0:
            m = jnp.maximum(m, jnp.max(s_ref[slot, 0:start, :], axis=0, keepdims=True))
        m_hi = jnp.maximum(m[:, 2 * tq:], jnp.max(s_b, axis=0, keepdims=True))
        m = jnp.concatenate([m[:, :2 * tq], m_hi], axis=1)
        p_a = jnp.exp2(s_a - m)
        l = jnp.sum(p_a, axis=0, keepdims=True)
        if start > 0:
            p_o = jnp.exp2(s_ref[slot, 0:start, :] - m)
            l = l + jnp.sum(p_o, axis=0, keepdims=True)
        p_b = jnp.exp2(s_b - m_hi)
        l_lo = l[:, :2 * tq]
        l_hi = l[:, 2 * tq:] + jnp.sum(p_b, axis=0, keepdims=True)
        ratio_lo = lam_full * l_lo[:, :tq] / l_lo[:, tq:]
        ratio_hi = lam_full * l_hi[:, :tq] / l_hi[:, tq:]

        def combine(p):
            return jnp.concatenate([p[:, 0:tq] - p[:, tq:2 * tq] * ratio_lo,
                                    p[:, 2 * tq:3 * tq] - p[:, 3 * tq:] * ratio_hi], axis=1).astype(_BF16)

        o_t = _dot(vt_ref[hh, :, start:mid], combine(p_a))
        if start > 0:
            o_t = o_t + _dot(vt_ref[hh, :, 0:start], combine(p_o))
        o_b = _dot(vt_ref[hh, :, mid:mid + tq], (p_b[:, :tq] - p_b[:, tq:] * ratio_hi).astype(_BF16))
        o_t = jnp.concatenate([o_t[:, :tq], o_t[:, tq:] + o_b], axis=1)
        inv = jnp.concatenate([1.0 / l_lo[:, :tq], 1.0 / l_hi[:, :tq]], axis=1)
        o = (o_t * inv).T
        o = o * lax.rsqrt(jnp.mean(o * o, axis=1, keepdims=True) + SUBLN_EPS) * subln
        o_ref[0, start:start + 2 * tq, cols] = o.astype(_BF16)

    work = [(hh, j) for j in range(seq // (2 * tq)) for hh in range(HEADS_PER_STEP)]
    scores(0, *work[0])
    for n, item in enumerate(work):
        if n + 1 < len(work):
            scores(n + 1, *work[n + 1])
        finish(n, *item)


def _attention(qkv, lam_stack, subln, j, lambda_init, batch, seq, d_model):
    n_steps = d_model // (2 * HEAD_DIM) // HEADS_PER_STEP
    width = HEADS_PER_STEP * LANES
    qkv = qkv.reshape(batch, seq, 4 * d_model)
    blk = lambda off: pl.BlockSpec((1, seq, width), lambda b, h: (b, 0, off + h))
    out = pl.pallas_call(
        functools.partial(_attn_kernel, seq=seq, lambda_init=lambda_init),
        grid=(batch, n_steps),
        in_specs=[blk(0), blk(n_steps), blk(2 * n_steps), blk(3 * n_steps),
                  _layer(lam_stack, j), _resident((1, LANES))],
        out_specs=pl.BlockSpec((1, seq, width), lambda b, h: (b, 0, h)),
        out_shape=jax.ShapeDtypeStruct((batch, seq, d_model), _BF16),
        scratch_shapes=[pltpu.VMEM((2, seq, 4 * Q_TILE), _F32),
                        pltpu.VMEM((HEADS_PER_STEP, LANES, seq), _BF16)],
        compiler_params=_params(("parallel", "parallel")),
        name="diff_attn",
    )(qkv, qkv, qkv, qkv, lam_stack, subln.reshape(1, LANES))
    return out.reshape(batch * seq, d_model)


def _ffn_kernel(*refs, has_proj, final, d_ff):
    refs = list(refs)
    x_ref = refs.pop(0)
    if has_proj:
        o_ref = refs.pop(0)
        wo_ref = refs.pop(0)
    g_ref, wgu_ref, wd_ref = refs.pop(0), refs.pop(0), refs.pop(0)
    if final:
        gf_ref = refs.pop(0)
    out_ref, act_ref = refs

    x = x_ref[...]
    if has_proj:
        x = x + _dot(o_ref[...], wo_ref[...])
    h = _rms(x, g_ref[...], NORM_EPS).astype(_BF16)
    fc = FF_CHUNK
    for j in range(d_ff // fc):
        gate = _dot(h, wgu_ref[:, j * fc:(j + 1) * fc])
        up = _dot(h, wgu_ref[:, d_ff + j * fc:d_ff + (j + 1) * fc])
        act = gate * (1.0 / (1.0 + jnp.exp(-gate))) * up
        act_ref[:, j * fc:(j + 1) * fc] = act.astype(_BF16)
    y = x + _dot(act_ref[...], wd_ref[...])
    if final:
        y = _rms(y, gf_ref[...], NORM_EPS)
    out_ref[...] = y


def _ffn(x, g, wgu_stack, wd_stack, layer, proj=None, final_g=None):
    m, d = x.shape
    d_ff = wd_stack.shape[1]
    tm = TOKEN_TILE
    row = lambda n: pl.BlockSpec((tm, n), lambda i: (i, 0))
    args, specs = [x], [row(d)]
    if proj is not None:
        o, wo_stack, j = proj
        args += [o, wo_stack]
        specs += [row(o.shape[1]), _layer(wo_stack, j)]
    args += [g, wgu_stack, wd_stack]
    specs += [_resident((1, d)), _layer(wgu_stack, layer), _layer(wd_stack, layer)]
    if final_g is not None:
        args.append(final_g)
        specs.append(_resident((1, d)))
    return pl.pallas_call(
        functools.partial(_ffn_kernel, has_proj=proj is not None,
                          final=final_g is not None, d_ff=d_ff),
        grid=(m // tm,),
        in_specs=specs,
        out_specs=row(d),
        out_shape=jax.ShapeDtypeStruct((m, d), _F32),
        scratch_shapes=[pltpu.VMEM((tm, d_ff), _BF16)],
        compiler_params=_params(("parallel",)),
        name="ffn",
    )(*args)


def _gmlp_kernel(x_ref, g_ref, win_ref, bin_ref, lng_ref, lnb_ref, ws_ref, bs_ref, wout_ref,
                 out_ref, u_ref, v_ref, vn_ref, gated_ref, *, width):
    tm = x_ref.shape[0]
    nc = 512
    gw = width // GMLP_GROUPS
    trow = lax.broadcasted_iota(jnp.int32, (CHUNK, CHUNK), 0)
    tcol = lax.broadcasted_iota(jnp.int32, (CHUNK, CHUNK), 1)
    tril = tcol <= trow
    ws = [jnp.where(tril, ws_ref[gi], 0.0).astype(_BF16) for gi in range(GMLP_GROUPS)]
    low_half = lax.broadcasted_iota(jnp.int32, (1, LANES), 1) < (gw - LANES)

    x = x_ref[...]
    h = _rms(x, g_ref[...], NORM_EPS).astype(_BF16)

    def in_proj(col):
        z = _dot(h, win_ref[:, col:col + nc]) + bin_ref[:, col:col + nc]
        return 0.5 * z * (1.0 + lax.erf(z * (1.0 / math.sqrt(2.0))))

    for col in range(0, width, nc):
        v_ref[:, col:col + nc] = in_proj(width + col)
    v = v_ref[...]
    mu = jnp.mean(v, axis=1, keepdims=True)
    vc = v - mu
    var = jnp.mean(vc * vc, axis=1, keepdims=True)
    vn_ref[...] = (vc * lax.rsqrt(var + LN_EPS) * lng_ref[...] + lnb_ref[...]).astype(_BF16)
    for col in range(0, width, nc):
        u_ref[:, col:col + nc] = in_proj(col)

    for c in range(tm // CHUNK):
        rows = slice(c * CHUNK, (c + 1) * CHUNK)
        for p in range(GMLP_GROUPS // 2):
            base = 2 * p * gw
            sa = _dot(ws[2 * p], vn_ref[rows, base:base + 2 * LANES])
            sb = _dot(ws[2 * p + 1], vn_ref[rows, base + LANES:base + 3 * LANES])
            pieces = (sa[:, :LANES],
                      jnp.where(low_half, sa[:, LANES:], sb[:, :LANES]),
                      sb[:, LANES:])
            for t, sv in enumerate(pieces):
                cols = slice(base + t * LANES, base + (t + 1) * LANES)
                gated_ref[rows, cols] = (u_ref[rows, cols] * (sv + bs_ref[:, cols])).astype(_BF16)
    out_ref[...] = x + _dot(gated_ref[...], wout_ref[...])


def _gmlp(x, g, w_in_stack, b_in, ln_g, ln_b, w_s, b_s, w_out_stack, j):
    m, d = x.shape
    width = w_out_stack.shape[1]
    tm = TOKEN_TILE
    gw = width // GMLP_GROUPS
    assert w_s.shape == (GMLP_GROUPS, CHUNK, CHUNK) and 2 * gw == 3 * LANES
    bias = jnp.broadcast_to(b_s.T[:, :, None], (CHUNK, GMLP_GROUPS, gw)).reshape(CHUNK, width)
    row = lambda n: pl.BlockSpec((tm, n), lambda i: (i, 0))
    return pl.pallas_call(
        functools.partial(_gmlp_kernel, width=width),
        grid=(m // tm,),
        in_specs=[row(d), _resident((1, d)), _layer(w_in_stack, j), _resident((1, 2 * width)),
                  _resident((1, width)), _resident((1, width)), _resident(w_s.shape),
                  _resident(bias.shape), _layer(w_out_stack, j)],
        out_specs=row(d),
        out_shape=jax.ShapeDtypeStruct((m, d), _F32),
        scratch_shapes=[pltpu.VMEM((tm, width), _F32), pltpu.VMEM((tm, width), _F32),
                        pltpu.VMEM((tm, width), _BF16), pltpu.VMEM((tm, width), _BF16)],
        compiler_params=_params(("parallel",)),
        name="gmlp",
    )(x, g, w_in_stack, b_in.reshape(1, -1), ln_g.reshape(1, -1), ln_b.reshape(1, -1), w_s, bias,
      w_out_stack)


def _conv_kernel(x_ref, g_ref, win_ref, cw_ref, wout_ref, out_ref, hc_ref, *, d_model, tiles_per_seq):
    tm = x_ref.shape[0]
    pad = 8

    @pl.when(pl.program_id(0) % tiles_per_seq == 0)
    def _():
        hc_ref[0:pad, :] = jnp.zeros((pad, d_model), _F32)

    x = x_ref[...]
    h = _rms(x, g_ref[...], NORM_EPS).astype(_BF16)
    gate_c = _dot(h, win_ref[:, d_model:2 * d_model])
    xs = _dot(h, win_ref[:, 2 * d_model:3 * d_model])
    hc = gate_c * xs
    hc_ref[pad:pad + tm, :] = hc
    conv = (cw_ref[0:1, :] * hc_ref[pad - 2:pad - 2 + tm, :]
            + cw_ref[1:2, :] * hc_ref[pad - 1:pad - 1 + tm, :]
            + cw_ref[2:3, :] * hc)
    hc_ref[0:pad, :] = hc_ref[tm:tm + pad, :]
    gate_b = _dot(h, win_ref[:, 0:d_model])
    out_ref[...] = x + _dot((gate_b * conv).astype(_BF16), wout_ref[...])


def _short_conv(x, g, w_in_stack, conv_w, w_out_stack, j, seq):
    m, d = x.shape
    tm = TOKEN_TILE
    row = lambda n: pl.BlockSpec((tm, n), lambda i: (i, 0))
    return pl.pallas_call(
        functools.partial(_conv_kernel, d_model=d, tiles_per_seq=seq // tm),
        grid=(m // tm,),
        in_specs=[row(d), _resident((1, d)), _layer(w_in_stack, j), _resident(conv_w.shape),
                  _layer(w_out_stack, j)],
        out_specs=row(d),
        out_shape=jax.ShapeDtypeStruct((m, d), _F32),
        scratch_shapes=[pltpu.VMEM((tm + 8, d), _F32)],
        compiler_params=_params(("arbitrary",)),
        name="short_conv",
    )(x, g, w_in_stack, conv_w, w_out_stack)


def kernel(x, positions, mix_norm, ffn_norm, final_norm, attn_w_in, attn_lambda, attn_subln, attn_w_out, gmlp_w_in, gmlp_b_in, gmlp_ln_g, gmlp_ln_b, gmlp_w_s, gmlp_b_s, gmlp_w_out, conv_w_in, conv_w, conv_w_out, ffn_w_gate_up, ffn_w_down):
    batch, seq, d = x.shape
    depth = mix_norm.shape[0]
    d_ff = ffn_w_down.shape[1]
    assert CONV_WIDTH == conv_w.shape[1] and seq % TOKEN_TILE == 0 and seq % (2 * Q_TILE) == 0
    assert d_ff % FF_CHUNK == 0 and TOKEN_TILE % CHUNK == 0
    assert (batch * seq) % (ROPE_ROWS * LANES // (HEAD_DIM // 2)) == 0
    assert d // (2 * HEAD_DIM) % HEADS_PER_STEP == 0

    xf = x.reshape(batch * seq, d)
    cos_t, sin_t = _rope_tables(positions)
    vec = lambda w: w.reshape(1, -1)
    attn_w_in, attn_w_out = attn_w_in.astype(_BF16), attn_w_out.astype(_BF16)
    gmlp_w_in, gmlp_w_out = gmlp_w_in.astype(_BF16), gmlp_w_out.astype(_BF16)
    conv_w_in, conv_w_out = conv_w_in.astype(_BF16), conv_w_out.astype(_BF16)
    ffn_w_gate_up, ffn_w_down = ffn_w_gate_up.astype(_BF16), ffn_w_down.astype(_BF16)

    for i in range(depth):
        kind, j = i % N_MIXERS, i // N_MIXERS
        proj = None
        if kind == 0:
            lambda_init = 0.8 - 0.6 * math.exp(-0.3 * i)
            qkv = _qkv_proj(xf, vec(mix_norm[i]), attn_w_in, j, cos_t, sin_t)
            o = _attention(qkv, attn_lambda, attn_subln[j], j, lambda_init, batch, seq, d)
            proj = (o, attn_w_out, j)
        elif kind == 1:
            xf = _gmlp(xf, vec(mix_norm[i]), gmlp_w_in, gmlp_b_in[j], gmlp_ln_g[j], gmlp_ln_b[j],
                       gmlp_w_s[j], gmlp_b_s[j], gmlp_w_out, j)
        else:
            xf = _short_conv(xf, vec(mix_norm[i]), conv_w_in, conv_w[j], conv_w_out, j, seq)
        xf = _ffn(xf, vec(ffn_norm[i]), ffn_w_gate_up, ffn_w_down, i, proj=proj,
                  final_g=vec(final_norm) if i == depth - 1 else None)
    return xf.reshape(batch, seq, d)
```
